```python
import math
import jax, jax.numpy as jnp
from jax import lax
import numpy as np

D_MODEL = 1024
BATCH = 8
SEQ = 2048
DEPTH = 4
DEC_BATCH = 32
DEC_SEQ = 1
PAST_LEN = 8192
PAGE_SIZE = 128

MIX_WIDTH = D_MODEL
GM_WIDTH = MIX_WIDTH // 2
GM_HEADS = 4
GM_HEAD_DIM = GM_WIDTH // GM_HEADS
CHUNK = 128
DA_WIDTH = MIX_WIDTH - GM_WIDTH
DA_HEADS = 4
DA_V_DIM = DA_WIDTH // DA_HEADS
DA_QK_DIM = DA_V_DIM // 2
ROT_DIM = DA_QK_DIM // 4
ROPE_THETA = 500000.0
D_FF = -(-8 * D_MODEL // (3 * 256)) * 256
IN_COLS = 2 * GM_WIDTH + 3 * DA_WIDTH
Q_BLOCK = 128
EPS = 1e-6

kernel_name = "hymba_gmlp_diffattn_decoder_step"


def _rmsnorm(x, g):
    xf = x.astype(jnp.float32)
    y = xf * lax.rsqrt(jnp.mean(xf * xf, axis=-1, keepdims=True) + EPS)
    return (y * g.astype(jnp.float32)).astype(x.dtype)


def _layernorm(x, g, b):
    xf = x.astype(jnp.float32)
    mu = jnp.mean(xf, axis=-1, keepdims=True)
    var = jnp.mean((xf - mu) ** 2, axis=-1, keepdims=True)
    y = (xf - mu) * lax.rsqrt(var + EPS)
    return (y * g.astype(jnp.float32) + b.astype(jnp.float32)).astype(x.dtype)


def _rope(x, pos):
    half = ROT_DIM // 2
    inv_freq = ROPE_THETA ** (-jnp.arange(0, ROT_DIM, 2, dtype=jnp.float32) / ROT_DIM)
    ang = pos.astype(jnp.float32)[:, None] * inv_freq[None, :]
    c = jnp.cos(ang)[None, :, None, None, :]
    s = jnp.sin(ang)[None, :, None, None, :]
    xf = x.astype(jnp.float32)
    x1, x2, rest = xf[..., :half], xf[..., half:ROT_DIM], xf[..., ROT_DIM:]
    out = jnp.concatenate([x1 * c - x2 * s, x2 * c + x1 * s, rest], axis=-1)
    return out.astype(x.dtype)


def _lambda_init(layer_idx):
    return 0.8 - 0.6 * math.exp(-0.3 * layer_idx)


def _mixer_inputs(x, pos, norm_pre, w_in_l, ln_g, ln_b):
    B, T, _ = x.shape
    h = _rmsnorm(x, norm_pre) @ w_in_l
    u = jax.nn.gelu(h[..., :GM_WIDTH])
    vg = _layernorm(jax.nn.gelu(h[..., GM_WIDTH:2 * GM_WIDTH]), ln_g, ln_b)
    o = 2 * GM_WIDTH
    q = h[..., o:o + DA_WIDTH].reshape(B, T, DA_HEADS, 2, DA_QK_DIM)
    k = h[..., o + DA_WIDTH:o + 2 * DA_WIDTH].reshape(B, T, DA_HEADS, 2, DA_QK_DIM)
    v = h[..., o + 2 * DA_WIDTH:].reshape(B, T, DA_HEADS, DA_V_DIM)
    return u, vg, _rope(q, pos), _rope(k, pos), v


def _diff_attend(q, k, v, mask, lam):
    s = jnp.einsum('bqhcd,bkhcd->bhcqk', q, k).astype(jnp.float32) * (DA_QK_DIM ** -0.5)
    s = jnp.where(mask, s, jnp.finfo(jnp.float32).min)
    p = jax.nn.softmax(s, axis=-1)
    a = (p[:, :, 0] - lam * p[:, :, 1]).astype(v.dtype)
    return jnp.einsum('bhqk,bkhd->bqhd', a, v)


def _diff_out(o, subln_g_l, lam_init):
    B, T = o.shape[:2]
    o = _rmsnorm(o, subln_g_l) * (1.0 - lam_init)
    return o.reshape(B, T, DA_WIDTH)


def _lambda(lq1, lk1, lq2, lk2, lam_init):
    f = jnp.float32
    return (jnp.exp(jnp.sum(lq1.astype(f) * lk1.astype(f)))
            - jnp.exp(jnp.sum(lq2.astype(f) * lk2.astype(f))) + lam_init)


def _finish_layer(x, gm, da, w_o_l, g_post_mix, g_pre_ffn, g_post_ffn, w_ffn_in_l, w_ffn_out_l):
    mix = jnp.concatenate([gm, da], axis=-1) @ w_o_l
    x = x + _rmsnorm(mix, g_post_mix)
    gu = _rmsnorm(x, g_pre_ffn) @ w_ffn_in_l
    f = (jax.nn.silu(gu[..., :D_FF]) * gu[..., D_FF:]) @ w_ffn_out_l
    return x + _rmsnorm(f, g_post_ffn)


def setup_inputs(seed: int = 0) -> dict:
    key = jax.random.key(seed)
    ks = jax.random.split(key, 24)
    n_pages = PAST_LEN // PAGE_SIZE
    n_pool = (DEC_BATCH * n_pages * 5) // 4
    f = jnp.float32
    nrm = lambda k, shp: jax.random.normal(k, shp, f)
    page_table = jax.random.permutation(ks[0], n_pool)[:DEC_BATCH * n_pages]
    page_table = page_table.reshape(DEC_BATCH, n_pages).astype(jnp.int32)
    return {
        "x_prompt": nrm(ks[1], (BATCH, SEQ, D_MODEL)),
        "x_sample": nrm(ks[2], (DEC_BATCH, DEC_SEQ, D_MODEL)),
        "cache_k": nrm(ks[3], (DEPTH, n_pool, PAGE_SIZE, DA_HEADS, 2 * DA_QK_DIM)),
        "cache_v": nrm(ks[4], (DEPTH, n_pool, PAGE_SIZE, DA_HEADS, DA_V_DIM)),
        "page_table": page_table,
        "w_in": nrm(ks[5], (DEPTH, D_MODEL, IN_COLS)) * D_MODEL ** -0.5,
        "w_gmlp_s": nrm(ks[6], (DEPTH, GM_HEADS, CHUNK, CHUNK)) * CHUNK ** -0.5,
        "b_gmlp_s": 1.0 + 0.1 * nrm(ks[7], (DEPTH, GM_HEADS, CHUNK)),
        "ln_v_g": 1.0 + 0.05 * nrm(ks[8], (DEPTH, GM_WIDTH)),
        "ln_v_b": 0.02 * nrm(ks[9], (DEPTH, GM_WIDTH)),
        "lambda_q1": 0.1 * nrm(ks[10], (DEPTH, DA_QK_DIM)),
        "lambda_k1": 0.1 * nrm(ks[11], (DEPTH, DA_QK_DIM)),
        "lambda_q2": 0.1 * nrm(ks[12], (DEPTH, DA_QK_DIM)),
        "lambda_k2": 0.1 * nrm(ks[13], (DEPTH, DA_QK_DIM)),
        "subln_g": 1.0 + 0.05 * nrm(ks[14], (DEPTH, DA_V_DIM)),
        "w_o": nrm(ks[15], (DEPTH, MIX_WIDTH, D_MODEL)) * MIX_WIDTH ** -0.5,
        "norm_g": 1.0 + 0.05 * nrm(ks[16], (DEPTH, 4, D_MODEL)),
        "w_ffn_in": nrm(ks[17], (DEPTH, D_MODEL, 2 * D_FF)) * D_MODEL ** -0.5,
        "w_ffn_out": nrm(ks[18], (DEPTH, D_FF, D_MODEL)) * D_FF ** -0.5,
    }


def reference(x_prompt, x_sample, cache_k, cache_v, page_table, w_in, w_gmlp_s, b_gmlp_s,
              ln_v_g, ln_v_b, lambda_q1, lambda_k1, lambda_q2, lambda_k2, subln_g, w_o,
              norm_g, w_ffn_in, w_ffn_out):
    n_pages = PAST_LEN // PAGE_SIZE
    tril = jnp.tril(jnp.ones((CHUNK, CHUNK), dtype=bool))
    pos_p = jnp.arange(SEQ, dtype=jnp.int32)
    rel_s = jnp.arange(DEC_SEQ, dtype=jnp.int32)
    pos_s = PAST_LEN + rel_s
    n_blocks = SEQ // Q_BLOCK
    ti = rel_s % CHUNK
    same_chunk = (rel_s[:, None] // CHUNK) == (rel_s[None, :] // CHUNK)
    gm_mask_s = same_chunk & (rel_s[:, None] >= rel_s[None, :])
    kidx = jnp.arange(PAST_LEN + DEC_SEQ, dtype=jnp.int32)
    att_mask_s = (kidx[None, :] < PAST_LEN) | ((kidx[None, :] - PAST_LEN) <= rel_s[:, None])

    xp, xs = x_prompt, x_sample
    k_p_out, v_p_out, k_s_out, v_s_out, gv_s_out = [], [], [], [], []
    for l in range(DEPTH):
        lam_init = _lambda_init(l)
        lam = _lambda(lambda_q1[l], lambda_k1[l], lambda_q2[l], lambda_k2[l], lam_init)
        ws_masked = w_gmlp_s[l] * tril

        B = xp.shape[0]
        u, vg, q, k, v = _mixer_inputs(xp, pos_p, norm_g[l, 0], w_in[l], ln_v_g[l], ln_v_b[l])
        vc = vg.reshape(B, SEQ // CHUNK, CHUNK, GM_HEADS, GM_HEAD_DIM)
        mixed = jnp.einsum('hts,bcshd->bcthd', ws_masked, vc) \
            + b_gmlp_s[l].T[None, None, :, :, None].astype(vc.dtype)
        gm_p = u * mixed.reshape(B, SEQ, GM_WIDTH)

        qb = jnp.moveaxis(q.reshape(B, n_blocks, Q_BLOCK, DA_HEADS, 2, DA_QK_DIM), 1, 0)
        qpb = pos_p.reshape(n_blocks, Q_BLOCK)

        def _block(args, k=k, v=v, lam=lam):
            q_blk, qpos = args
            return _diff_attend(q_blk, k, v, pos_p[None, :] <= qpos[:, None], lam)

        o_p = lax.map(_block, (qb, qpb))
        o_p = jnp.moveaxis(o_p, 0, 1).reshape(B, SEQ, DA_HEADS, DA_V_DIM)
        da_p = _diff_out(o_p, subln_g[l], lam_init)
        k_p_out.append(k.reshape(B, SEQ, DA_HEADS, 2 * DA_QK_DIM))
        v_p_out.append(v)
        xp = _finish_layer(xp, gm_p, da_p, w_o[l], norm_g[l, 1], norm_g[l, 2], norm_g[l, 3],
                           w_ffn_in[l], w_ffn_out[l])

        Bs = xs.shape[0]
        u, vg, q, k, v = _mixer_inputs(xs, pos_s, norm_g[l, 0], w_in[l], ln_v_g[l], ln_v_b[l])
        wg = w_gmlp_s[l][:, ti[:, None], ti[None, :]] * gm_mask_s
        mixed = jnp.einsum('hts,bshd->bthd', wg, vg.reshape(Bs, DEC_SEQ, GM_HEADS, GM_HEAD_DIM)) \
            + b_gmlp_s[l][:, ti].T[None, :, :, None].astype(vg.dtype)
        gm_s = u * mixed.reshape(Bs, DEC_SEQ, GM_WIDTH)

        k_new = k.reshape(Bs, DEC_SEQ, DA_HEADS, 2 * DA_QK_DIM)
        k_past = cache_k[l][page_table].reshape(Bs, n_pages * PAGE_SIZE, DA_HEADS, 2 * DA_QK_DIM)
        v_past = cache_v[l][page_table].reshape(Bs, n_pages * PAGE_SIZE, DA_HEADS, DA_V_DIM)
        k_all = jnp.concatenate([k_past, k_new.astype(k_past.dtype)], axis=1)
        v_all = jnp.concatenate([v_past, v.astype(v_past.dtype)], axis=1)
        k_all = k_all.reshape(Bs, PAST_LEN + DEC_SEQ, DA_HEADS, 2, DA_QK_DIM)
        o_s = _diff_attend(q, k_all, v_all, att_mask_s, lam)
        da_s = _diff_out(o_s, subln_g[l], lam_init)
        k_s_out.append(k_new)
        v_s_out.append(v)
        gv_s_out.append(vg)
        xs = _finish_layer(xs, gm_s, da_s, w_o[l], norm_g[l, 1], norm_g[l, 2], norm_g[l, 3],
                           w_ffn_in[l], w_ffn_out[l])

    k_prompt = jnp.stack(k_p_out)
    v_prompt = jnp.stack(v_p_out)
    k_sample = jnp.stack(k_s_out)
    v_sample = jnp.stack(v_s_out)
    gmlp_v_sample = jnp.stack(gv_s_out)
    return (xp, xs, k_prompt, v_prompt, k_sample, v_sample, gmlp_v_sample)
```

```python
import functools
import math

import jax
import jax.numpy as jnp
from jax import lax
from jax.experimental import pallas as pl
from jax.experimental.pallas import tpu as pltpu

D_MODEL = 1024
SEQ = 2048
DEPTH = 4
PAST_LEN = 8192
PAGE_SIZE = 128
GM_WIDTH = 512
GM_HEADS = 4
GM_HEAD_DIM = 128
CHUNK = 128
DA_WIDTH = 512
DA_HEADS = 4
DA_V_DIM = 128
DA_QK_DIM = 64
ROT_DIM = 16
ROPE_THETA = 500000.0
D_FF = 2816
IN_COLS = 2 * GM_WIDTH + 3 * DA_WIDTH
EPS = 1e-6

LANES = 128
MXU_DIM = 256
VMEM_LIMIT_BYTES = 56 * 1024 * 1024

ROW_TILE = 512
Q_TILE = 256
FF_CHUNKS = ((0, 1536), (1536, 1280))
PAGES_PER_STEP = 8

F32 = jnp.float32
BF16 = jnp.bfloat16


def _lambda_init(layer_idx):
    return 0.8 - 0.6 * math.exp(-0.3 * layer_idx)


def _rms(x, g):
    return (x * lax.rsqrt(jnp.mean(x * x, axis=-1, keepdims=True) + EPS)) * g


def _dot(a, b):
    return jnp.dot(a, b, preferred_element_type=F32)


def _dot_nt(a, b):
    return lax.dot_general(a, b, (((1,), (1,)), ((), ())), preferred_element_type=F32)


def _lambda_vec(lq1_ref, lk1_ref, lq2_ref, lk2_ref, lam_init):
    a = jnp.sum(lq1_ref[...] * lk1_ref[...], axis=-1, keepdims=True)
    b = jnp.sum(lq2_ref[...] * lk2_ref[...], axis=-1, keepdims=True)
    return jnp.exp(a) - jnp.exp(b) + lam_init


def _const_spec(shape):
    nd = len(shape)
    return pl.BlockSpec(shape, lambda *_: (0,) * nd, pipeline_mode=pl.Buffered(1))


def _mixer_in_kernel(x_ref, g_ref, w_ref, lng_ref, lnb_ref, ws_ref, bst_ref,
                     cos_ref, sa_ref, sb_ref, *out_refs, sample):
    if sample:
        gm_ref, q_ref, k_ref, v_ref, vg_ref = out_refs
    else:
        gm_ref, q_ref, k_ref, v_ref, kb_ref, vb_ref = out_refs
    rows = x_ref.shape[0]
    xn = _rms(x_ref[...], g_ref[...]).astype(BF16)

    def proj(c0, n):
        return _dot(xn, w_ref[:, c0:c0 + n])

    u = jax.nn.gelu(proj(0, GM_WIDTH))
    vr = jax.nn.gelu(proj(GM_WIDTH, GM_WIDTH))
    mu = jnp.mean(vr, axis=-1, keepdims=True)
    var = jnp.mean((vr - mu) ** 2, axis=-1, keepdims=True)
    vg = (vr - mu) * lax.rsqrt(var + EPS) * lng_ref[...] + lnb_ref[...]

    if sample:
        vg_ref[...] = vg
        for h in range(GM_HEADS):
            hs = slice(h * GM_HEAD_DIM, (h + 1) * GM_HEAD_DIM)
            w00 = ws_ref[h, 0:1, 0:1]
            b00 = bst_ref[0:1, h:h + 1]
            mixed = vg[:, hs] * w00 + b00
            gm_ref[:, hs] = (u[:, hs] * mixed).astype(BF16)
    else:
        r = lax.broadcasted_iota(jnp.int32, (CHUNK, CHUNK), 0)
        c = lax.broadcasted_iota(jnp.int32, (CHUNK, CHUNK), 1)
        tril = r >= c
        vgb = vg.astype(BF16)
        for h in range(GM_HEADS):
            hs = slice(h * GM_HEAD_DIM, (h + 1) * GM_HEAD_DIM)
            wsm = jnp.where(tril, ws_ref[h], 0.0).astype(BF16)
            bcol = bst_ref[:, h:h + 1]
            for ci in range(rows // CHUNK):
                rs = slice(ci * CHUNK, (ci + 1) * CHUNK)
                mixed = _dot(wsm, vgb[rs, hs]) + bcol
                gm_ref[rs, hs] = (u[rs, hs] * mixed).astype(BF16)

    cos = cos_ref[...]
    sa = sa_ref[...]
    sb = sb_ref[...]

    def rope(t):
        return (t * cos + pltpu.roll(t, LANES - ROT_DIM // 2, 1) * sa
                + pltpu.roll(t, ROT_DIM // 2, 1) * sb)

    o = 2 * GM_WIDTH
    qf = proj(o, DA_WIDTH)
    for h in range(DA_HEADS):
        hs = slice(h * DA_V_DIM, (h + 1) * DA_V_DIM)
        q_ref[:, hs] = (rope(qf[:, hs]) * (DA_QK_DIM ** -0.5)).astype(q_ref.dtype)
    kf = proj(o + DA_WIDTH, DA_WIDTH)
    for h in range(DA_HEADS):
        hs = slice(h * DA_V_DIM, (h + 1) * DA_V_DIM)
        kr = rope(kf[:, hs])
        k_ref[:, hs] = kr
        if not sample:
            kb_ref[:, hs] = kr.astype(BF16)
    vf = proj(o + 2 * DA_WIDTH, DA_WIDTH)
    v_ref[...] = vf
    if not sample:
        vb_ref[...] = vf.astype(BF16)


def _mixer_in(x, g, w_in, lng, lnb, ws, bst, cos, sa, sb, *, sample):
    n = x.shape[0]
    tm = n if sample else ROW_TILE
    assert n % tm == 0 and (sample or SEQ % tm == 0)
    steps = n // tm
    tab_steps = 1 if sample else SEQ // tm
    row = lambda width: pl.BlockSpec((tm, width), lambda i: (i, 0))
    tab = pl.BlockSpec((cos.shape[0] // tab_steps, LANES), lambda i: (i % tab_steps, 0))
    in_specs = [
        row(D_MODEL), _const_spec((1, D_MODEL)), _const_spec((D_MODEL, IN_COLS)),
        _const_spec((1, GM_WIDTH)), _const_spec((1, GM_WIDTH)),
        _const_spec((GM_HEADS, CHUNK, CHUNK)), _const_spec((CHUNK, GM_HEADS)),
        tab, tab, tab,
    ]
    sds = jax.ShapeDtypeStruct
    if sample:
        out_shape = [sds((n, GM_WIDTH), BF16), sds((n, DA_WIDTH), F32), sds((n, DA_WIDTH), F32),
                     sds((n, DA_WIDTH), F32), sds((n, GM_WIDTH), F32)]
    else:
        out_shape = [sds((n, GM_WIDTH), BF16), sds((n, DA_WIDTH), BF16), sds((n, DA_WIDTH), F32),
                     sds((n, DA_WIDTH), F32), sds((n, DA_WIDTH), BF16), sds((n, DA_WIDTH), BF16)]
    out_specs = [row(s.shape[1]) for s in out_shape]
    return pl.pallas_call(
        functools.partial(_mixer_in_kernel, sample=sample),
        grid=(steps,), in_specs=in_specs, out_specs=out_specs, out_shape=out_shape,
        compiler_params=pltpu.CompilerParams(
            dimension_semantics=("arbitrary",), vmem_limit_bytes=VMEM_LIMIT_BYTES),
        name="mixer_in_sample" if sample else "mixer_in_prompt",
    )(x, g, w_in, lng, lnb, ws, bst, cos, sa, sb)


def _prompt_attn_kernel(q_ref, k_ref, v_ref, lq1_ref, lk1_ref, lq2_ref, lk2_ref, sg_ref, o_ref,
                        *, lam_init):
    lam = _lambda_vec(lq1_ref, lk1_ref, lq2_ref, lk2_ref, lam_init)
    tq = Q_TILE
    lane = lax.broadcasted_iota(jnp.int32, (tq, LANES), 1)
    first_half = lane < DA_QK_DIM
    rr = lax.broadcasted_iota(jnp.int32, (tq, tq), 0)
    cc = lax.broadcasted_iota(jnp.int32, (tq, tq), 1)
    causal = rr >= cc
    neg = jnp.finfo(F32).min
    zero = jnp.zeros((), BF16)
    for i in range(SEQ // tq):
        past = i * tq
        qi = q_ref[past:past + tq, :]
        kd = k_ref[past:past + tq, :]
        vd = v_ref[past:past + tq, :]
        outs = []
        for qc in (jnp.where(first_half, qi, zero), jnp.where(first_half, zero, qi)):
            sd = jnp.where(causal, _dot_nt(qc, kd), neg)
            m = jnp.max(sd, axis=-1, keepdims=True)
            if past:
                sp = _dot_nt(qc, k_ref[0:past, :])
                m = jnp.maximum(m, jnp.max(sp, axis=-1, keepdims=True))
            pd = jnp.exp(sd - m)
            l = jnp.sum(pd, axis=-1, keepdims=True)
            acc = _dot(pd.astype(BF16), vd)
            if past:
                pp = jnp.exp(sp - m)
                l = l + jnp.sum(pp, axis=-1, keepdims=True)
                acc = acc + _dot(pp.astype(BF16), v_ref[0:past, :])
            outs.append(acc / l)
        o = outs[0] - lam * outs[1]
        y = _rms(o, sg_ref[...]) * (1.0 - lam_init)
        o_ref[past:past + tq, :] = y.astype(o_ref.dtype)


def _prompt_attn(q, kb, vb, lq1, lk1, lq2, lk2, sg, *, lam_init):
    n = q.shape[0]
    batch = n // SEQ
    head = pl.BlockSpec((SEQ, DA_V_DIM), lambda b, h: (b, h))
    vec = _const_spec((1, DA_QK_DIM))
    return pl.pallas_call(
        functools.partial(_prompt_attn_kernel, lam_init=lam_init),
        grid=(batch, DA_HEADS),
        in_specs=[head, head, head, vec, vec, vec, vec, _const_spec((1, DA_V_DIM))],
        out_specs=head,
        out_shape=jax.ShapeDtypeStruct((n, DA_WIDTH), BF16),
        compiler_params=pltpu.CompilerParams(
            dimension_semantics=("arbitrary", "arbitrary"), vmem_limit_bytes=VMEM_LIMIT_BYTES),
        name="prompt_attn",
    )(q, kb, vb, lq1, lk1, lq2, lk2, sg)


def _decode_attn_kernel(pt_ref, q_ref, kn_ref, vn_ref, lq1_ref, lk1_ref, lq2_ref, lk2_ref, sg_ref,
                        *refs, lam_init):
    del pt_ref
    npg = PAGES_PER_STEP
    k_refs = refs[:npg]
    v_refs = refs[npg:2 * npg]
    o_ref = refs[2 * npg]
    m_ref, l_ref, acc_ref = refs[2 * npg + 1:]
    j = pl.program_id(1)
    nrow = 2 * DA_HEADS

    rid = lax.broadcasted_iota(jnp.int32, (nrow, DA_WIDTH), 0)
    cid = lax.broadcasted_iota(jnp.int32, (nrow, DA_WIDTH), 1)
    own = lax.shift_right_logical(cid, int(math.log2(DA_QK_DIM))) == rid
    qbd = jnp.where(own, jnp.broadcast_to(q_ref[...], (nrow, DA_WIDTH)), 0.0)

    @pl.when(j == 0)
    def _():
        m_ref[...] = jnp.full(m_ref.shape, jnp.finfo(F32).min, F32)
        l_ref[...] = jnp.zeros(l_ref.shape, F32)
        acc_ref[...] = jnp.zeros(acc_ref.shape, F32)

    qb = qbd.astype(BF16)
    s = [_dot_nt(qb, k_refs[p][...].astype(BF16)) for p in range(npg)]
    m_old = m_ref[...]
    m_new = m_old
    for sp in s:
        m_new = jnp.maximum(m_new, jnp.max(sp, axis=-1, keepdims=True))
    alpha = jnp.exp(m_old - m_new)
    l_new = alpha * l_ref[...]
    acc = alpha * acc_ref[...]
    for p in range(npg):
        pr = jnp.exp(s[p] - m_new)
        l_new = l_new + jnp.sum(pr, axis=-1, keepdims=True)
        acc = acc + _dot(pr.astype(BF16), v_refs[p][...].astype(BF16))
    m_ref[...] = m_new
    l_ref[...] = l_new
    acc_ref[...] = acc

    @pl.when(j == pl.num_programs(1) - 1)
    def _():
        lam = _lambda_vec(lq1_ref, lk1_ref, lq2_ref, lk2_ref, lam_init)
        s_new = jnp.sum(qbd * kn_ref[...], axis=-1, keepdims=True)
        m_f = jnp.maximum(m_new, s_new)
        a = jnp.exp(m_new - m_f)
        p_new = jnp.exp(s_new - m_f)
        l_f = a * l_new + p_new
        full = (a * acc + p_new * vn_ref[...]) / l_f
        for h in range(DA_HEADS):
            hs = slice(h * DA_V_DIM, (h + 1) * DA_V_DIM)
            o = full[2 * h:2 * h + 1, hs] - lam * full[2 * h + 1:2 * h + 2, hs]
            o_ref[:, hs] = (_rms(o, sg_ref[...]) * (1.0 - lam_init)).astype(o_ref.dtype)


def _decode_attn(page_table, q, k_new, v_new, cache_k, cache_v, lq1, lk1, lq2, lk2, sg,
                 *, layer, lam_init):
    bs = q.shape[0]
    n_pages = page_table.shape[1]
    npg = PAGES_PER_STEP
    assert n_pages % npg == 0
    rowspec = pl.BlockSpec((None, 1, DA_WIDTH), lambda b, j, pt: (b, 0, 0))
    vec = pl.BlockSpec((1, DA_QK_DIM), lambda b, j, pt: (0, 0))

    def page_spec(p):
        return pl.BlockSpec((None, None, PAGE_SIZE, DA_WIDTH),
                            lambda b, j, pt: (layer, pt[b, j * npg + p], 0, 0))

    pages = [page_spec(p) for p in range(npg)]
    nrow = 2 * DA_HEADS
    grid_spec = pltpu.PrefetchScalarGridSpec(
        num_scalar_prefetch=1,
        grid=(bs, n_pages // npg),
        in_specs=[rowspec, rowspec, rowspec, vec, vec, vec, vec,
                  pl.BlockSpec((1, DA_V_DIM), lambda b, j, pt: (0, 0))] + pages + pages,
        out_specs=rowspec,
        scratch_shapes=[pltpu.VMEM((nrow, 1), F32), pltpu.VMEM((nrow, 1), F32),
                        pltpu.VMEM((nrow, DA_WIDTH), F32)],
    )
    out = pl.pallas_call(
        functools.partial(_decode_attn_kernel, lam_init=lam_init),
        grid_spec=grid_spec,
        out_shape=jax.ShapeDtypeStruct((bs, 1, DA_WIDTH), BF16),
        compiler_params=pltpu.CompilerParams(
            dimension_semantics=("arbitrary", "arbitrary"), vmem_limit_bytes=VMEM_LIMIT_BYTES),
        name="decode_attn",
    )(page_table, q.reshape(bs, 1, DA_WIDTH), k_new.reshape(bs, 1, DA_WIDTH),
      v_new.reshape(bs, 1, DA_WIDTH), lq1, lk1, lq2, lk2, sg,
      *([cache_k] * npg), *([cache_v] * npg))
    return out.reshape(bs, DA_WIDTH)


def _finish_kernel(x_ref, gm_ref, da_ref, wo_ref, g1_ref, g2_ref, g3_ref, wi_ref, wf_ref, o_ref):
    mix = _dot(gm_ref[...], wo_ref[0:GM_WIDTH, :]) + _dot(da_ref[...], wo_ref[GM_WIDTH:, :])
    x1 = x_ref[...] + _rms(mix, g1_ref[...])
    xn = _rms(x1, g2_ref[...]).astype(BF16)
    f = None
    for c0, n in FF_CHUNKS:
        gate = _dot(xn, wi_ref[:, c0:c0 + n])
        up = _dot(xn, wi_ref[:, D_FF + c0:D_FF + c0 + n])
        act = (jax.nn.silu(gate) * up).astype(BF16)
        part = _dot(act, wf_ref[c0:c0 + n, :])
        f = part if f is None else f + part
    o_ref[...] = x1 + _rms(f, g3_ref[...])


def _finish(x, gm, da, w_o, g1, g2, g3, w_ffn_in, w_ffn_out, *, tm):
    n = x.shape[0]
    assert n % tm == 0
    row = lambda width: pl.BlockSpec((tm, width), lambda i: (i, 0))
    gain = _const_spec((1, D_MODEL))
    return pl.pallas_call(
        _finish_kernel,
        grid=(n // tm,),
        in_specs=[row(D_MODEL), row(GM_WIDTH), row(DA_WIDTH), _const_spec((D_MODEL, D_MODEL)),
                  gain, gain, gain, _const_spec((D_MODEL, 2 * D_FF)), _const_spec((D_FF, D_MODEL))],
        out_specs=row(D_MODEL),
        out_shape=jax.ShapeDtypeStruct((n, D_MODEL), F32),
        compiler_params=pltpu.CompilerParams(
            dimension_semantics=("arbitrary",), vmem_limit_bytes=VMEM_LIMIT_BYTES),
        name="finish_rows%d" % tm,
    )(x, gm, da, w_o, g1, g2, g3, w_ffn_in, w_ffn_out)


def _rope_tables(pos):
    half = ROT_DIM // 2
    inv_freq = ROPE_THETA ** (-jnp.arange(0, ROT_DIM, 2, dtype=F32) / ROT_DIM)
    ang = pos.astype(F32)[:, None] * inv_freq[None, :]
    cos, sin = jnp.cos(ang), jnp.sin(ang)
    npos = pos.shape[0]
    rest = DA_QK_DIM - ROT_DIM
    ones = jnp.ones((npos, rest), F32)
    zeros = jnp.zeros((npos, rest), F32)
    zh = jnp.zeros((npos, half), F32)
    c64 = jnp.concatenate([cos, cos, ones], axis=1)
    a64 = jnp.concatenate([-sin, zh, zeros], axis=1)
    b64 = jnp.concatenate([zh, sin, zeros], axis=1)
    two = lambda t: jnp.concatenate([t, t], axis=1)
    return two(c64), two(a64), two(b64)


def kernel(x_prompt, x_sample, cache_k, cache_v, page_table, w_in, w_gmlp_s, b_gmlp_s, ln_v_g, ln_v_b,
           lambda_q1, lambda_k1, lambda_q2, lambda_k2, subln_g, w_o, norm_g, w_ffn_in, w_ffn_out):
    batch, seq, _ = x_prompt.shape
    bs = x_sample.shape[0]
    n_pool = cache_k.shape[1]
    xp = x_prompt.reshape(batch * seq, D_MODEL)
    xs = x_sample.reshape(bs, D_MODEL)
    ck = cache_k.reshape(DEPTH, n_pool, PAGE_SIZE, DA_WIDTH)
    cv = cache_v.reshape(DEPTH, n_pool, PAGE_SIZE, DA_WIDTH)
    tabs_p = _rope_tables(jnp.arange(SEQ, dtype=jnp.int32))
    tabs_s = _rope_tables(jnp.full((1,), PAST_LEN, dtype=jnp.int32))
    w_in_b = w_in.astype(BF16)
    w_o_b = w_o.astype(BF16)
    w_fi_b = w_ffn_in.astype(BF16)
    w_fo_b = w_ffn_out.astype(BF16)

    k_p, v_p, k_s, v_s, gv_s = [], [], [], [], []
    for l in range(DEPTH):
        lam_init = _lambda_init(l)
        g = lambda i: norm_g[l, i].reshape(1, D_MODEL)
        lng = ln_v_g[l].reshape(1, GM_WIDTH)
        lnb = ln_v_b[l].reshape(1, GM_WIDTH)
        bst = b_gmlp_s[l].T
        lams = [t[l].reshape(1, DA_QK_DIM) for t in (lambda_q1, lambda_k1, lambda_q2, lambda_k2)]
        sg = subln_g[l].reshape(1, DA_V_DIM)

        gm, q, k, v, kb, vb = _mixer_in(xp, g(0), w_in_b[l], lng, lnb, w_gmlp_s[l], bst, *tabs_p,
                                        sample=False)
        da = _prompt_attn(q, kb, vb, *lams, sg, lam_init=lam_init)
        xp = _finish(xp, gm, da, w_o_b[l], g(1), g(2), g(3), w_fi_b[l], w_fo_b[l], tm=ROW_TILE)
        k_p.append(k.reshape(batch, seq, DA_HEADS, 2 * DA_QK_DIM))
        v_p.append(v.reshape(batch, seq, DA_HEADS, DA_V_DIM))

        gm, q, k, v, vg = _mixer_in(xs, g(0), w_in_b[l], lng, lnb, w_gmlp_s[l], bst, *tabs_s,
                                    sample=True)
        da = _decode_attn(page_table, q, k, v, ck, cv, *lams, sg, layer=l, lam_init=lam_init)
        xs = _finish(xs, gm, da, w_o_b[l], g(1), g(2), g(3), w_fi_b[l], w_fo_b[l], tm=bs)
        k_s.append(k.reshape(bs, 1, DA_HEADS, 2 * DA_QK_DIM))
        v_s.append(v.reshape(bs, 1, DA_HEADS, DA_V_DIM))
        gv_s.append(vg.reshape(bs, 1, GM_WIDTH))

    return (xp.reshape(batch, seq, D_MODEL), xs.reshape(bs, 1, D_MODEL),
            jnp.stack(k_p), jnp.stack(v_p), jnp.stack(k_s), jnp.stack(v_s), jnp.stack(gv_s))
```

```python
import functools
import math

import jax
import jax.numpy as jnp
from jax import lax
from jax.experimental import pallas as pl
from jax.experimental.pallas import tpu as pltpu

D_MODEL = 1024
SEQ = 2048
DEPTH = 4
PAST_LEN = 8192
PAGE_SIZE = 128
GM_WIDTH = 512
GM_HEADS = 4
GM_HEAD_DIM = 128
CHUNK = 128
DA_WIDTH = 512
DA_HEADS = 4
DA_V_DIM = 128
DA_QK_DIM = 64
ROT_DIM = 16
ROPE_THETA = 500000.0
D_FF = 2816
IN_COLS = 2 * GM_WIDTH + 3 * DA_WIDTH
EPS = 1e-6

LANES = 128
MXU_DIM = 256
VMEM_LIMIT_BYTES = 56 * 1024 * 1024

ROW_TILE = 512
Q_TILE = 256
FF_CHUNKS = ((0, 1536), (1536, 1280))
PAGES_PER_STEP = 16
PAGE_ROWS = PAGE_SIZE * DA_HEADS

Q_SCALE = (DA_QK_DIM ** -0.5) * math.log2(math.e)

F32 = jnp.float32
BF16 = jnp.bfloat16


def _lambda_init(layer_idx):
    return 0.8 - 0.6 * math.exp(-0.3 * layer_idx)


def _rms(x, g):
    return (x * lax.rsqrt(jnp.mean(x * x, axis=-1, keepdims=True) + EPS)) * g


def _dot(a, b):
    return jnp.dot(a, b, preferred_element_type=F32)


def _dot_nt(a, b):
    return lax.dot_general(a, b, (((1,), (1,)), ((), ())), preferred_element_type=F32)


def _lambda_vec(lq1_ref, lk1_ref, lq2_ref, lk2_ref, lam_init):
    a = jnp.sum(lq1_ref[...] * lk1_ref[...], axis=-1, keepdims=True)
    b = jnp.sum(lq2_ref[...] * lk2_ref[...], axis=-1, keepdims=True)
    return jnp.exp(a) - jnp.exp(b) + lam_init


def _layer_spec(shape, layer, grid_rank):
    nd = len(shape)
    if grid_rank == 1:
        index_map = lambda i: (layer,) + (0,) * nd
    else:
        index_map = lambda i, j: (layer,) + (0,) * nd
    return pl.BlockSpec((None,) + tuple(shape), index_map, pipeline_mode=pl.Buffered(1))


def _params(grid_rank):
    return pltpu.CompilerParams(dimension_semantics=("arbitrary",) * grid_rank,
                                vmem_limit_bytes=VMEM_LIMIT_BYTES)


def _mixer_in_kernel(*refs, sample, n_alias):
    (x_ref, g_ref, w_ref, lng_ref, lnb_ref, ws_ref, bst_ref, cos_ref, sa_ref, sb_ref) = refs[:10]
    out_refs = refs[10 + n_alias:]
    if sample:
        gm_ref, q_ref, k_ref, v_ref, vg_ref = out_refs
    else:
        gm_ref, q_ref, kb_ref, vb_ref, k_ref, v_ref = out_refs
    rows = x_ref.shape[0]
    xn = _rms(x_ref[...], g_ref[...]).astype(BF16)

    def proj(c0, n):
        return _dot(xn, w_ref[:, c0:c0 + n])

    u = jax.nn.gelu(proj(0, GM_WIDTH))
    vr = jax.nn.gelu(proj(GM_WIDTH, GM_WIDTH))
    mu = jnp.mean(vr, axis=-1, keepdims=True)
    var = jnp.mean((vr - mu) ** 2, axis=-1, keepdims=True)
    vg = (vr - mu) * lax.rsqrt(var + EPS) * lng_ref[...] + lnb_ref[...]

    if sample:
        vg_ref[...] = vg
        for h in range(GM_HEADS):
            hs = slice(h * GM_HEAD_DIM, (h + 1) * GM_HEAD_DIM)
            w00 = ws_ref[h, 0:1, 0:1]
            b00 = bst_ref[0:1, h:h + 1]
            mixed = vg[:, hs] * w00 + b00
            gm_ref[:, hs] = (u[:, hs] * mixed).astype(BF16)
    else:
        r = lax.broadcasted_iota(jnp.int32, (CHUNK, CHUNK), 0)
        c = lax.broadcasted_iota(jnp.int32, (CHUNK, CHUNK), 1)
        tril = r >= c
        vgb = vg.astype(BF16)
        for h in range(GM_HEADS):
            hs = slice(h * GM_HEAD_DIM, (h + 1) * GM_HEAD_DIM)
            wsm = jnp.where(tril, ws_ref[h], 0.0).astype(BF16)
            bcol = bst_ref[:, h:h + 1]
            for ci in range(rows // CHUNK):
                rs = slice(ci * CHUNK, (ci + 1) * CHUNK)
                mixed = _dot(wsm, vgb[rs, hs]) + bcol
                gm_ref[rs, hs] = (u[rs, hs] * mixed).astype(BF16)

    cos = cos_ref[...]
    sa = sa_ref[...]
    sb = sb_ref[...]

    def rope(t):
        return (t * cos + pltpu.roll(t, LANES - ROT_DIM // 2, 1) * sa
                + pltpu.roll(t, ROT_DIM // 2, 1) * sb)

    o = 2 * GM_WIDTH
    qf = proj(o, DA_WIDTH)
    for h in range(DA_HEADS):
        hs = slice(h * DA_V_DIM, (h + 1) * DA_V_DIM)
        q_ref[:, hs] = (rope(qf[:, hs]) * Q_SCALE).astype(q_ref.dtype)
    kf = proj(o + DA_WIDTH, DA_WIDTH)
    vf = proj(o + 2 * DA_WIDTH, DA_WIDTH)
    for h in range(DA_HEADS):
        hs = slice(h * DA_V_DIM, (h + 1) * DA_V_DIM)
        kr = rope(kf[:, hs])
        if sample:
            k_ref[:, hs] = kr
        else:
            k_ref[pl.ds(h, rows, stride=DA_HEADS), :] = kr
            v_ref[pl.ds(h, rows, stride=DA_HEADS), :] = vf[:, hs]
            kb_ref[:, hs] = kr.astype(BF16)
    if sample:
        v_ref[...] = vf
    else:
        vb_ref[...] = vf.astype(BF16)


def _mixer_in(x, norm_g, w_in, lng, lnb, ws, bst, cos, sa, sb, *, layer, sample, kv_prev=None):
    n = x.shape[0]
    tm = n if sample else ROW_TILE
    assert n % tm == 0 and (sample or SEQ % tm == 0)
    steps = n // tm
    tab_steps = 1 if sample else SEQ // tm
    row = lambda width: pl.BlockSpec((tm, width), lambda i: (i, 0))
    tab = pl.BlockSpec((cos.shape[0] // tab_steps, LANES), lambda i: (i % tab_steps, 0))
    in_specs = [
        row(D_MODEL),
        pl.BlockSpec((None, 1, D_MODEL), lambda i: (4 * layer, 0, 0), pipeline_mode=pl.Buffered(1)),
        _layer_spec((D_MODEL, IN_COLS), layer, 1),
        _layer_spec((1, GM_WIDTH), layer, 1), _layer_spec((1, GM_WIDTH), layer, 1),
        _layer_spec((GM_HEADS, CHUNK, CHUNK), layer, 1), _layer_spec((CHUNK, GM_HEADS), layer, 1),
        tab, tab, tab,
    ]
    args = [x, norm_g, w_in, lng, lnb, ws, bst, cos, sa, sb]
    sds = jax.ShapeDtypeStruct
    aliases = {}
    if sample:
        out_shape = [sds((n, GM_WIDTH), BF16), sds((n, DA_WIDTH), F32), sds((n, DA_WIDTH), F32),
                     sds((n, DA_WIDTH), F32), sds((n, GM_WIDTH), F32)]
        out_specs = [row(s.shape[1]) for s in out_shape]
    else:
        small = [sds((n, GM_WIDTH), BF16), sds((n, DA_WIDTH), BF16), sds((n, DA_WIDTH), BF16),
                 sds((n, DA_WIDTH), BF16)]
        big = sds((DEPTH, n * DA_HEADS, DA_V_DIM), F32)
        out_shape = small + [big, big]
        big_spec = pl.BlockSpec((None, tm * DA_HEADS, DA_V_DIM), lambda i: (layer, i, 0))
        out_specs = [row(s.shape[1]) for s in small] + [big_spec, big_spec]
        if kv_prev is not None:
            for t in kv_prev:
                aliases[len(args)] = len(small) + len(aliases)
                args.append(t)
                in_specs.append(pl.BlockSpec(memory_space=pl.ANY))
    return pl.pallas_call(
        functools.partial(_mixer_in_kernel, sample=sample, n_alias=len(aliases)),
        grid=(steps,), in_specs=in_specs, out_specs=out_specs, out_shape=out_shape,
        input_output_aliases=aliases, compiler_params=_params(1),
        name="mixer_in_sample" if sample else "mixer_in_prompt",
    )(*args)


def _prompt_attn_kernel(q_ref, k_ref, v_ref, lq1_ref, lk1_ref, lq2_ref, lk2_ref, sg_ref, o_ref,
                        vaug_ref, *, lam_init):
    lam = _lambda_vec(lq1_ref, lk1_ref, lq2_ref, lk2_ref, lam_init)
    tq = Q_TILE
    vaug_ref[:, 0:DA_V_DIM] = v_ref[...]
    vaug_ref[:, DA_V_DIM:] = jnp.ones((SEQ, MXU_DIM - DA_V_DIM), BF16)
    lane = lax.broadcasted_iota(jnp.int32, (tq, LANES), 1)
    first_half = lane < DA_QK_DIM
    rr = lax.broadcasted_iota(jnp.int32, (tq, tq), 0)
    cc = lax.broadcasted_iota(jnp.int32, (tq, tq), 1)
    causal = rr >= cc
    neg = jnp.finfo(F32).min
    zero = jnp.zeros((), BF16)
    for i in range(SEQ // tq):
        past = i * tq
        qi = q_ref[past:past + tq, :]
        outs = []
        for qc in (jnp.where(first_half, qi, zero), jnp.where(first_half, zero, qi)):
            s = _dot_nt(qc, k_ref[0:past + tq, :])
            sd = jnp.where(causal, s[:, past:], neg)
            m = jnp.max(sd, axis=-1, keepdims=True)
            if past:
                m = jnp.maximum(m, jnp.max(s[:, :past], axis=-1, keepdims=True))
            p = jnp.exp2(sd - m).astype(BF16)
            if past:
                p = jnp.concatenate([jnp.exp2(s[:, :past] - m).astype(BF16), p], axis=1)
            acc = _dot(p, vaug_ref[0:past + tq, :])
            outs.append(acc[:, 0:DA_V_DIM] / acc[:, DA_V_DIM:DA_V_DIM + 1])
        o = outs[0] - lam * outs[1]
        y = _rms(o, sg_ref[...]) * (1.0 - lam_init)
        o_ref[past:past + tq, :] = y.astype(o_ref.dtype)


def _prompt_attn(q, kb, vb, lq1, lk1, lq2, lk2, sg, *, layer, lam_init):
    n = q.shape[0]
    batch = n // SEQ
    head = pl.BlockSpec((SEQ, DA_V_DIM), lambda b, h: (b, h))
    vec = _layer_spec((1, DA_QK_DIM), layer, 2)
    return pl.pallas_call(
        functools.partial(_prompt_attn_kernel, lam_init=lam_init),
        grid=(batch, DA_HEADS),
        in_specs=[head, head, head, vec, vec, vec, vec, _layer_spec((1, DA_V_DIM), layer, 2)],
        out_specs=head,
        out_shape=jax.ShapeDtypeStruct((n, DA_WIDTH), BF16),
        scratch_shapes=[pltpu.VMEM((SEQ, MXU_DIM), BF16)],
        compiler_params=_params(2),
        name="prompt_attn",
    )(q, kb, vb, lq1, lk1, lq2, lk2, sg)


def _decode_attn_kernel(pt_ref, q_ref, kn_ref, vn_ref, lq1_ref, lk1_ref, lq2_ref, lk2_ref, sg_ref,
                        *refs, lam_init):
    del pt_ref
    npg = PAGES_PER_STEP
    k_refs = refs[:npg]
    v_refs = refs[npg:2 * npg]
    o_ref = refs[2 * npg]
    m_ref, l_ref, acc_ref = refs[2 * npg + 1:]
    j = pl.program_id(1)
    nrow = 2 * DA_HEADS

    rid = lax.broadcasted_iota(jnp.int32, (nrow, LANES), 0)
    lid = lax.broadcasted_iota(jnp.int32, (nrow, LANES), 1)
    row_head = lax.shift_right_logical(rid, 1)

    def per_row_head(vec_ref):
        out = jnp.zeros((nrow, LANES), F32)
        for h in range(DA_HEADS):
            hs = slice(h * LANES, (h + 1) * LANES)
            out = jnp.where(row_head == h, jnp.broadcast_to(vec_ref[:, hs], (nrow, LANES)), out)
        return out

    own_half = lax.shift_right_logical(lid, int(math.log2(DA_QK_DIM))) == (rid & 1)
    qm = jnp.where(own_half, per_row_head(q_ref), 0.0)

    cid = lax.broadcasted_iota(jnp.int32, (nrow, PAGE_ROWS), 1)
    rid_w = lax.broadcasted_iota(jnp.int32, (nrow, PAGE_ROWS), 0)
    own_col = (cid & (DA_HEADS - 1)) == lax.shift_right_logical(rid_w, 1)
    neg = jnp.finfo(F32).min

    @pl.when(j == 0)
    def _():
        m_ref[...] = jnp.full(m_ref.shape, neg, F32)
        l_ref[...] = jnp.zeros(l_ref.shape, F32)
        acc_ref[...] = jnp.zeros(acc_ref.shape, F32)

    qb = qm.astype(BF16)
    s = [jnp.where(own_col, _dot_nt(qb, k_refs[p][...].astype(BF16)), neg) for p in range(npg)]
    m_old = m_ref[...]
    m_new = m_old
    for sp in s:
        m_new = jnp.maximum(m_new, jnp.max(sp, axis=-1, keepdims=True))
    alpha = jnp.exp2(m_old - m_new)
    l_new = alpha * l_ref[...]
    acc = alpha * acc_ref[...]
    for p in range(npg):
        pr = jnp.exp2(s[p] - m_new)
        l_new = l_new + jnp.sum(pr, axis=-1, keepdims=True)
        acc = acc + _dot(pr.astype(BF16), v_refs[p][...].astype(BF16))
    m_ref[...] = m_new
    l_ref[...] = l_new
    acc_ref[...] = acc

    @pl.when(j == pl.num_programs(1) - 1)
    def _():
        lam = _lambda_vec(lq1_ref, lk1_ref, lq2_ref, lk2_ref, lam_init)
        s_new = jnp.sum(qm * per_row_head(kn_ref), axis=-1, keepdims=True)
        m_f = jnp.maximum(m_new, s_new)
        a = jnp.exp2(m_new - m_f)
        p_new = jnp.exp2(s_new - m_f)
        l_f = a * l_new + p_new
        full = (a * acc + p_new * per_row_head(vn_ref)) / l_f
        for h in range(DA_HEADS):
            hs = slice(h * DA_V_DIM, (h + 1) * DA_V_DIM)
            o = full[2 * h:2 * h + 1, :] - lam * full[2 * h + 1:2 * h + 2, :]
            o_ref[:, hs] = (_rms(o, sg_ref[...]) * (1.0 - lam_init)).astype(o_ref.dtype)


def _decode_attn(page_table, q, k_new, v_new, cache_k, cache_v, lq1, lk1, lq2, lk2, sg,
                 *, layer, lam_init):
    bs = q.shape[0]
    n_pages = page_table.shape[1]
    npg = PAGES_PER_STEP
    assert n_pages % npg == 0
    rowspec = pl.BlockSpec((None, 1, DA_WIDTH), lambda b, j, pt: (b, 0, 0))
    vec = pl.BlockSpec((None, 1, DA_QK_DIM), lambda b, j, pt: (layer, 0, 0))

    def page_spec(p):
        return pl.BlockSpec((None, None, PAGE_ROWS, DA_V_DIM),
                            lambda b, j, pt: (layer, pt[b, j * npg + p], 0, 0))

    pages = [page_spec(p) for p in range(npg)]
    nrow = 2 * DA_HEADS
    grid_spec = pltpu.PrefetchScalarGridSpec(
        num_scalar_prefetch=1,
        grid=(bs, n_pages // npg),
        in_specs=[rowspec, rowspec, rowspec, vec, vec, vec, vec,
                  pl.BlockSpec((None, 1, DA_V_DIM), lambda b, j, pt: (layer, 0, 0))] + pages + pages,
        out_specs=rowspec,
        scratch_shapes=[pltpu.VMEM((nrow, 1), F32), pltpu.VMEM((nrow, 1), F32),
                        pltpu.VMEM((nrow, DA_V_DIM), F32)],
    )
    out = pl.pallas_call(
        functools.partial(_decode_attn_kernel, lam_init=lam_init),
        grid_spec=grid_spec,
        out_shape=jax.ShapeDtypeStruct((bs, 1, DA_WIDTH), BF16),
        compiler_params=_params(2),
        name="decode_attn",
    )(page_table, q.reshape(bs, 1, DA_WIDTH), k_new.reshape(bs, 1, DA_WIDTH),
      v_new.reshape(bs, 1, DA_WIDTH), lq1, lk1, lq2, lk2, sg,
      *([cache_k] * npg), *([cache_v] * npg))
    return out.reshape(bs, DA_WIDTH)


def _finish_kernel(x_ref, gm_ref, da_ref, wo_ref, g1_ref, g2_ref, g3_ref, wi_ref, wf_ref, o_ref):
    mix = _dot(gm_ref[...], wo_ref[0:GM_WIDTH, :]) + _dot(da_ref[...], wo_ref[GM_WIDTH:, :])
    x1 = x_ref[...] + _rms(mix, g1_ref[...])
    xn = _rms(x1, g2_ref[...]).astype(BF16)
    f = None
    for c0, n in FF_CHUNKS:
        gate = _dot(xn, wi_ref[:, c0:c0 + n])
        up = _dot(xn, wi_ref[:, D_FF + c0:D_FF + c0 + n])
        act = (jax.nn.silu(gate) * up).astype(BF16)
        part = _dot(act, wf_ref[c0:c0 + n, :])
        f = part if f is None else f + part
    o_ref[...] = x1 + _rms(f, g3_ref[...])


def _finish(x, gm, da, w_o, norm_g, w_ffn_in, w_ffn_out, *, layer, tm):
    n = x.shape[0]
    assert n % tm == 0
    row = lambda width: pl.BlockSpec((tm, width), lambda i: (i, 0))
    gain = lambda k: pl.BlockSpec((None, 1, D_MODEL), lambda i: (4 * layer + k, 0, 0),
                                  pipeline_mode=pl.Buffered(1))
    return pl.pallas_call(
        _finish_kernel,
        grid=(n // tm,),
        in_specs=[row(D_MODEL), row(GM_WIDTH), row(DA_WIDTH),
                  _layer_spec((D_MODEL, D_MODEL), layer, 1), gain(1), gain(2), gain(3),
                  _layer_spec((D_MODEL, 2 * D_FF), layer, 1), _layer_spec((D_FF, D_MODEL), layer, 1)],
        out_specs=row(D_MODEL),
        out_shape=jax.ShapeDtypeStruct((n, D_MODEL), F32),
        compiler_params=_params(1),
        name="finish_rows%d" % tm,
    )(x, gm, da, w_o, norm_g, norm_g, norm_g, w_ffn_in, w_ffn_out)


def _rope_tables(pos):
    half = ROT_DIM // 2
    inv_freq = ROPE_THETA ** (-jnp.arange(0, ROT_DIM, 2, dtype=F32) / ROT_DIM)
    ang = pos.astype(F32)[:, None] * inv_freq[None, :]
    cos, sin = jnp.cos(ang), jnp.sin(ang)
    npos = pos.shape[0]
    rest = DA_QK_DIM - ROT_DIM
    ones = jnp.ones((npos, rest), F32)
    zeros = jnp.zeros((npos, rest), F32)
    zh = jnp.zeros((npos, half), F32)
    c64 = jnp.concatenate([cos, cos, ones], axis=1)
    a64 = jnp.concatenate([-sin, zh, zeros], axis=1)
    b64 = jnp.concatenate([zh, sin, zeros], axis=1)
    two = lambda t: jnp.concatenate([t, t], axis=1)
    return two(c64), two(a64), two(b64)


def kernel(x_prompt, x_sample, cache_k, cache_v, page_table, w_in, w_gmlp_s, b_gmlp_s, ln_v_g, ln_v_b,
           lambda_q1, lambda_k1, lambda_q2, lambda_k2, subln_g, w_o, norm_g, w_ffn_in, w_ffn_out):
    batch, seq, _ = x_prompt.shape
    bs = x_sample.shape[0]
    n_pool = cache_k.shape[1]
    xp = x_prompt.reshape(batch * seq, D_MODEL)
    xs = x_sample.reshape(bs, D_MODEL)
    ck = cache_k.reshape(DEPTH, n_pool, PAGE_ROWS, DA_V_DIM)
    cv = cache_v.reshape(DEPTH, n_pool, PAGE_ROWS, DA_V_DIM)
    tabs_p = _rope_tables(jnp.arange(SEQ, dtype=jnp.int32))
    tabs_s = _rope_tables(jnp.full((1,), PAST_LEN, dtype=jnp.int32))
    w_in_b = w_in.astype(BF16)
    w_o_b = w_o.astype(BF16)
    w_fi_b = w_ffn_in.astype(BF16)
    w_fo_b = w_ffn_out.astype(BF16)
    gains = norm_g.reshape(DEPTH * 4, 1, D_MODEL)
    lng = ln_v_g.reshape(DEPTH, 1, GM_WIDTH)
    lnb = ln_v_b.reshape(DEPTH, 1, GM_WIDTH)
    bst = jnp.swapaxes(b_gmlp_s, 1, 2)
    lams = [t.reshape(DEPTH, 1, DA_QK_DIM) for t in (lambda_q1, lambda_k1, lambda_q2, lambda_k2)]
    sg = subln_g.reshape(DEPTH, 1, DA_V_DIM)

    kv_p = None
    k_s, v_s, gv_s = [], [], []
    for l in range(DEPTH):
        lam_init = _lambda_init(l)
        gm, q, kb, vb, k_all, v_all = _mixer_in(xp, gains, w_in_b, lng, lnb, w_gmlp_s, bst, *tabs_p,
                                                layer=l, sample=False, kv_prev=kv_p)
        kv_p = (k_all, v_all)
        da = _prompt_attn(q, kb, vb, *lams, sg, layer=l, lam_init=lam_init)
        xp = _finish(xp, gm, da, w_o_b, gains, w_fi_b, w_fo_b, layer=l, tm=ROW_TILE)

        gm, q, k, v, vg = _mixer_in(xs, gains, w_in_b, lng, lnb, w_gmlp_s, bst, *tabs_s,
                                    layer=l, sample=True)
        da = _decode_attn(page_table, q, k, v, ck, cv, *lams, sg, layer=l, lam_init=lam_init)
        xs = _finish(xs, gm, da, w_o_b, gains, w_fi_b, w_fo_b, layer=l, tm=bs)
        k_s.append(k.reshape(bs, 1, DA_HEADS, 2 * DA_QK_DIM))
        v_s.append(v.reshape(bs, 1, DA_HEADS, DA_V_DIM))
        gv_s.append(vg.reshape(bs, 1, GM_WIDTH))

    k_prompt = kv_p[0].reshape(DEPTH, batch, seq, DA_HEADS, 2 * DA_QK_DIM)
    v_prompt = kv_p[1].reshape(DEPTH, batch, seq, DA_HEADS, DA_V_DIM)
    return (xp.reshape(batch, seq, D_MODEL), xs.reshape(bs, 1, D_MODEL),
            k_prompt, v_prompt, jnp.stack(k_s), jnp.stack(v_s), jnp.stack(gv_s))
```

```python
import functools
import math

import jax
import jax.numpy as jnp
from jax import lax
from jax.experimental import pallas as pl
from jax.experimental.pallas import tpu as pltpu

D_MODEL = 1024
SEQ = 2048
DEPTH = 4
PAST_LEN = 8192
PAGE_SIZE = 128
GM_WIDTH = 512
GM_HEADS = 4
GM_HEAD_DIM = 128
CHUNK = 128
DA_WIDTH = 512
DA_HEADS = 4
DA_V_DIM = 128
DA_QK_DIM = 64
ROT_DIM = 16
ROPE_THETA = 500000.0
D_FF = 2816
IN_COLS = 2 * GM_WIDTH + 3 * DA_WIDTH
EPS = 1e-6

LANES = 128
MXU_DIM = 256
VMEM_LIMIT_BYTES = 56 * 1024 * 1024

ROW_TILE = 512
Q_TILE = 256
FF_CHUNKS = ((0, 512), (512, 512), (1024, 512), (1536, 512), (2048, 256), (2304, 256), (2560, 256))
PAGE_ROWS = PAGE_SIZE * DA_HEADS

Q_SCALE = (DA_QK_DIM ** -0.5) * math.log2(math.e)

F32 = jnp.float32
BF16 = jnp.bfloat16


def _lambda_init(layer_idx):
    return 0.8 - 0.6 * math.exp(-0.3 * layer_idx)


def _rms(x, g):
    return (x * lax.rsqrt(jnp.mean(x * x, axis=-1, keepdims=True) + EPS)) * g


def _dot(a, b):
    return jnp.dot(a, b, preferred_element_type=F32)


def _dot_nt(a, b):
    return lax.dot_general(a, b, (((1,), (1,)), ((), ())), preferred_element_type=F32)


def _lambda_vec(lq1_ref, lk1_ref, lq2_ref, lk2_ref, lam_init):
    a = jnp.sum(lq1_ref[...] * lk1_ref[...], axis=-1, keepdims=True)
    b = jnp.sum(lq2_ref[...] * lk2_ref[...], axis=-1, keepdims=True)
    return jnp.exp(a) - jnp.exp(b) + lam_init


def _layer_spec(shape, layer, grid_rank):
    nd = len(shape)
    if grid_rank == 1:
        index_map = lambda i: (layer,) + (0,) * nd
    else:
        index_map = lambda i, j: (layer,) + (0,) * nd
    return pl.BlockSpec((None,) + tuple(shape), index_map, pipeline_mode=pl.Buffered(1))


def _params(grid_rank):
    return pltpu.CompilerParams(dimension_semantics=("arbitrary",) * grid_rank,
                                vmem_limit_bytes=VMEM_LIMIT_BYTES)


def _mixer_in_kernel(*refs, sample, n_alias, layer):
    (x_ref, g_ref, w_ref, lng_ref, lnb_ref, ws_ref, bst_ref, cos_ref, sa_ref, sb_ref) = refs[:10]
    out_refs = refs[10 + n_alias:]
    if sample:
        gm_ref, q_ref, k_ref, v_ref, vg_ref = out_refs
    else:
        gm_ref, q_ref, kb_ref, vb_ref, k_ref, v_ref = out_refs
        k_out, v_out = (k_ref, v_ref) if n_alias else (k_ref.at[layer], v_ref.at[layer])
    rows = x_ref.shape[0]
    xn = _rms(x_ref[...], g_ref[...]).astype(BF16)

    def proj(c0, n):
        return _dot(xn, w_ref[:, c0:c0 + n])

    u = jax.nn.gelu(proj(0, GM_WIDTH))
    vr = jax.nn.gelu(proj(GM_WIDTH, GM_WIDTH))
    mu = jnp.mean(vr, axis=-1, keepdims=True)
    var = jnp.mean((vr - mu) ** 2, axis=-1, keepdims=True)
    vg = (vr - mu) * lax.rsqrt(var + EPS) * lng_ref[...] + lnb_ref[...]

    if sample:
        vg_ref[...] = vg
        for h in range(GM_HEADS):
            hs = slice(h * GM_HEAD_DIM, (h + 1) * GM_HEAD_DIM)
            w00 = ws_ref[h, 0:1, 0:1]
            b00 = bst_ref[0:1, h:h + 1]
            mixed = vg[:, hs] * w00 + b00
            gm_ref[:, hs] = (u[:, hs] * mixed).astype(BF16)
    else:
        r = lax.broadcasted_iota(jnp.int32, (CHUNK, CHUNK), 0)
        c = lax.broadcasted_iota(jnp.int32, (CHUNK, CHUNK), 1)
        tril = r >= c
        vgb = vg.astype(BF16)
        for h in range(GM_HEADS):
            hs = slice(h * GM_HEAD_DIM, (h + 1) * GM_HEAD_DIM)
            wsm = jnp.where(tril, ws_ref[h], 0.0).astype(BF16)
            bcol = bst_ref[:, h:h + 1]
            for ci in range(rows // CHUNK):
                rs = slice(ci * CHUNK, (ci + 1) * CHUNK)
                mixed = _dot(wsm, vgb[rs, hs]) + bcol
                gm_ref[rs, hs] = (u[rs, hs] * mixed).astype(BF16)

    cos = cos_ref[...]
    sa = sa_ref[...]
    sb = sb_ref[...]

    def rope(t):
        return (t * cos + pltpu.roll(t, LANES - ROT_DIM // 2, 1) * sa
                + pltpu.roll(t, ROT_DIM // 2, 1) * sb)

    o = 2 * GM_WIDTH
    qf = proj(o, DA_WIDTH)
    for h in range(DA_HEADS):
        hs = slice(h * DA_V_DIM, (h + 1) * DA_V_DIM)
        q_ref[:, hs] = (rope(qf[:, hs]) * Q_SCALE).astype(q_ref.dtype)
    kf = proj(o + DA_WIDTH, DA_WIDTH)
    vf = proj(o + 2 * DA_WIDTH, DA_WIDTH)
    for h in range(DA_HEADS):
        hs = slice(h * DA_V_DIM, (h + 1) * DA_V_DIM)
        kr = rope(kf[:, hs])
        if sample:
            k_ref[:, hs] = kr
        else:
            k_out[pl.ds(h, rows, stride=DA_HEADS), :] = kr
            v_out[pl.ds(h, rows, stride=DA_HEADS), :] = vf[:, hs]
            kb_ref[:, hs] = kr.astype(BF16)
    if sample:
        v_ref[...] = vf
    else:
        vb_ref[...] = vf.astype(BF16)
        if n_alias == 0:
            for other in range(DEPTH):
                if other != layer:
                    k_ref[other] = jnp.zeros(k_ref.shape[1:], F32)
                    v_ref[other] = jnp.zeros(v_ref.shape[1:], F32)


def _mixer_in(x, norm_g, w_in, lng, lnb, ws, bst, cos, sa, sb, *, layer, sample, kv_prev=None):
    n = x.shape[0]
    tm = n if sample else ROW_TILE
    assert n % tm == 0 and (sample or SEQ % tm == 0)
    steps = n // tm
    tab_steps = 1 if sample else SEQ // tm
    row = lambda width: pl.BlockSpec((tm, width), lambda i: (i, 0))
    tab = pl.BlockSpec((cos.shape[0] // tab_steps, LANES), lambda i: (i % tab_steps, 0))
    in_specs = [
        row(D_MODEL),
        pl.BlockSpec((None, 1, D_MODEL), lambda i: (4 * layer, 0, 0), pipeline_mode=pl.Buffered(1)),
        _layer_spec((D_MODEL, IN_COLS), layer, 1),
        _layer_spec((1, GM_WIDTH), layer, 1), _layer_spec((1, GM_WIDTH), layer, 1),
        _layer_spec((GM_HEADS, CHUNK, CHUNK), layer, 1), _layer_spec((CHUNK, GM_HEADS), layer, 1),
        tab, tab, tab,
    ]
    args = [x, norm_g, w_in, lng, lnb, ws, bst, cos, sa, sb]
    sds = jax.ShapeDtypeStruct
    aliases = {}
    if sample:
        out_shape = [sds((n, GM_WIDTH), BF16), sds((n, DA_WIDTH), F32), sds((n, DA_WIDTH), F32),
                     sds((n, DA_WIDTH), F32), sds((n, GM_WIDTH), F32)]
        out_specs = [row(s.shape[1]) for s in out_shape]
    else:
        small = [sds((n, GM_WIDTH), BF16), sds((n, DA_WIDTH), BF16), sds((n, DA_WIDTH), BF16),
                 sds((n, DA_WIDTH), BF16)]
        big = sds((DEPTH, n * DA_HEADS, DA_V_DIM), F32)
        out_shape = small + [big, big]
        if kv_prev is None:
            big_spec = pl.BlockSpec((DEPTH, tm * DA_HEADS, DA_V_DIM), lambda i: (0, i, 0))
        else:
            big_spec = pl.BlockSpec((None, tm * DA_HEADS, DA_V_DIM), lambda i: (layer, i, 0))
            for t in kv_prev:
                aliases[len(args)] = len(small) + len(aliases)
                args.append(t)
                in_specs.append(pl.BlockSpec(memory_space=pl.ANY))
        out_specs = [row(s.shape[1]) for s in small] + [big_spec, big_spec]
    return pl.pallas_call(
        functools.partial(_mixer_in_kernel, sample=sample, n_alias=len(aliases), layer=layer),
        grid=(steps,), in_specs=in_specs, out_specs=out_specs, out_shape=out_shape,
        input_output_aliases=aliases, compiler_params=_params(1),
        name="mixer_in_sample" if sample else "mixer_in_prompt",
    )(*args)


def _prompt_attn_kernel(q_ref, k_ref, v_ref, lq1_ref, lk1_ref, lq2_ref, lk2_ref, sg_ref, o_ref,
                        vaug_ref, *, lam_init):
    lam = _lambda_vec(lq1_ref, lk1_ref, lq2_ref, lk2_ref, lam_init)
    tq = Q_TILE
    vaug_ref[:, 0:DA_V_DIM] = v_ref[...]
    vaug_ref[:, DA_V_DIM:] = jnp.ones((SEQ, MXU_DIM - DA_V_DIM), BF16)
    lane = lax.broadcasted_iota(jnp.int32, (tq, LANES), 1)
    first_half = lane < DA_QK_DIM
    rr = lax.broadcasted_iota(jnp.int32, (tq, tq), 0)
    cc = lax.broadcasted_iota(jnp.int32, (tq, tq), 1)
    causal = rr >= cc
    neg = jnp.finfo(F32).min
    zero = jnp.zeros((), BF16)
    for i in reversed(range(SEQ // tq)):
        past = i * tq
        qi = q_ref[past:past + tq, :]
        outs = []
        for qc in (jnp.where(first_half, qi, zero), jnp.where(first_half, zero, qi)):
            s = _dot_nt(qc, k_ref[0:past + tq, :])
            sd = jnp.where(causal, s[:, past:], neg)
            m = jnp.max(sd, axis=-1, keepdims=True)
            if past:
                m = jnp.maximum(m, jnp.max(s[:, :past], axis=-1, keepdims=True))
            p = jnp.exp2(sd - m).astype(BF16)
            if past:
                p = jnp.concatenate([jnp.exp2(s[:, :past] - m).astype(BF16), p], axis=1)
            acc = _dot(p, vaug_ref[0:past + tq, :])
            outs.append(acc[:, 0:DA_V_DIM] / acc[:, DA_V_DIM:DA_V_DIM + 1])
        o = outs[0] - lam * outs[1]
        y = _rms(o, sg_ref[...]) * (1.0 - lam_init)
        o_ref[past:past + tq, :] = y.astype(o_ref.dtype)


def _prompt_attn(q, kb, vb, lq1, lk1, lq2, lk2, sg, *, layer, lam_init):
    n = q.shape[0]
    batch = n // SEQ
    head = pl.BlockSpec((SEQ, DA_V_DIM), lambda b, h: (b, h))
    vec = _layer_spec((1, DA_QK_DIM), layer, 2)
    return pl.pallas_call(
        functools.partial(_prompt_attn_kernel, lam_init=lam_init),
        grid=(batch, DA_HEADS),
        in_specs=[head, head, head, vec, vec, vec, vec, _layer_spec((1, DA_V_DIM), layer, 2)],
        out_specs=head,
        out_shape=jax.ShapeDtypeStruct((n, DA_WIDTH), BF16),
        scratch_shapes=[pltpu.VMEM((SEQ, MXU_DIM), BF16)],
        compiler_params=_params(2),
        name="prompt_attn",
    )(q, kb, vb, lq1, lk1, lq2, lk2, sg)


N_SCORE_ROWS = 2 * DA_HEADS
NEG = float(jnp.finfo(jnp.float32).min)


def _per_row_head(vec_ref):
    rid = lax.broadcasted_iota(jnp.int32, (N_SCORE_ROWS, LANES), 0)
    row_head = lax.shift_right_logical(rid, 1)
    out = jnp.zeros((N_SCORE_ROWS, LANES), F32)
    for h in range(DA_HEADS):
        hs = slice(h * LANES, (h + 1) * LANES)
        out = jnp.where(row_head == h, jnp.broadcast_to(vec_ref[:, hs], (N_SCORE_ROWS, LANES)), out)
    return out


def _decode_query(q_ref):
    rid = lax.broadcasted_iota(jnp.int32, (N_SCORE_ROWS, LANES), 0)
    lid = lax.broadcasted_iota(jnp.int32, (N_SCORE_ROWS, LANES), 1)
    own_half = lax.shift_right_logical(lid, int(math.log2(DA_QK_DIM))) == (rid & 1)
    return jnp.where(own_half, _per_row_head(q_ref), 0.0)


def _decode_query_rows(q_ref):
    rid = lax.broadcasted_iota(jnp.int32, (8, LANES), 0)
    out = jnp.zeros((8, LANES), F32)
    for h in range(DA_HEADS):
        hs = slice(h * LANES, (h + 1) * LANES)
        out = jnp.where((rid & (DA_HEADS - 1)) == h, jnp.broadcast_to(q_ref[:, hs], (8, LANES)), out)
    return out


def _decode_scores(q8, k_pages, state):
    cid = lax.broadcasted_iota(jnp.int32, (N_SCORE_ROWS, PAGE_ROWS), 1)
    rid = lax.broadcasted_iota(jnp.int32, (N_SCORE_ROWS, PAGE_ROWS), 0)
    own_col = (cid & (DA_HEADS - 1)) == lax.shift_right_logical(rid, 1)
    first_half = (rid & 1) == 0
    sub = q8.shape[0]

    def scores(kp):
        prod = (kp.reshape(PAGE_ROWS // sub, sub, LANES) * q8[None]).reshape(PAGE_ROWS, LANES)
        t = prod.T
        s1 = jnp.sum(t[0:DA_QK_DIM], axis=0, keepdims=True)
        s2 = jnp.sum(t[DA_QK_DIM:], axis=0, keepdims=True)
        return jnp.where(own_col, jnp.where(first_half, s1, s2), NEG)

    m_old, l_old, acc_old = state
    s = [scores(kp) for kp in k_pages]
    m_new = m_old
    for sp in s:
        m_new = jnp.maximum(m_new, jnp.max(sp, axis=-1, keepdims=True))
    alpha = jnp.exp2(m_old - m_new)
    l_new = alpha * l_old
    probs = []
    for sp in s:
        pr = jnp.exp2(sp - m_new)
        l_new = l_new + jnp.sum(pr, axis=-1, keepdims=True)
        probs.append(pr.astype(BF16))
    return (m_new, l_new, alpha * acc_old), probs


def _decode_values(state, probs, v_pages):
    m_new, l_new, acc = state
    for pr, vp in zip(probs, v_pages):
        acc = acc + _dot(pr, vp.astype(BF16))
    return m_new, l_new, acc


def _decode_finish(qm, state, kn_ref, vn_ref, lam, sg_ref, o_ref, lam_init):
    m_old, l_old, acc = state
    s_new = jnp.sum(qm * _per_row_head(kn_ref), axis=-1, keepdims=True)
    m_f = jnp.maximum(m_old, s_new)
    a = jnp.exp2(m_old - m_f)
    p_new = jnp.exp2(s_new - m_f)
    l_f = a * l_old + p_new
    full = (a * acc + p_new * _per_row_head(vn_ref)) / l_f
    for h in range(DA_HEADS):
        hs = slice(h * DA_V_DIM, (h + 1) * DA_V_DIM)
        o = full[2 * h:2 * h + 1, :] - lam * full[2 * h + 1:2 * h + 2, :]
        o_ref[:, hs] = (_rms(o, sg_ref[...]) * (1.0 - lam_init)).astype(o_ref.dtype)


def _finish_kernel(*refs, decode, layer, lam_init):
    if decode:
        pt_ref, refs = refs[0], refs[1:]
        (q_ref, kn_ref, vn_ref, lq1_ref, lk1_ref, lq2_ref, lk2_ref, sg_ref, ck_hbm, cv_hbm,
         o_ref, das_ref, kbuf, vbuf, sem) = refs[9:]
    else:
        (o_ref,) = refs[9:]
    x_ref, gm_ref, da_ref, wo_ref, g1_ref, g2_ref, g3_ref, wi_ref, wf_ref = refs[:9]
    vals = {}

    def project_and_norm(between_matmuls):
        mix = _dot(gm_ref[...], wo_ref[0:GM_WIDTH, :]) + _dot(da_ref[...], wo_ref[GM_WIDTH:, :])
        between_matmuls()
        vals["x1"] = x_ref[...] + _rms(mix, g1_ref[...])
        vals["xn"] = _rms(vals["x1"], g2_ref[...]).astype(BF16)

    def ffn_chunk(c0, n, last, between_matmuls):
        gate = _dot(vals["xn"], wi_ref[:, c0:c0 + n])
        up = _dot(vals["xn"], wi_ref[:, D_FF + c0:D_FF + c0 + n])
        between_matmuls()
        act = (jax.nn.silu(gate) * up).astype(BF16)
        part = _dot(act, wf_ref[c0:c0 + n, :])
        vals["f"] = part if c0 == 0 else vals["f"] + part
        if last:
            o_ref[...] = vals["x1"] + _rms(vals["f"], g3_ref[...])

    pieces = [project_and_norm] + [
        functools.partial(ffn_chunk, c0, n, k == len(FF_CHUNKS) - 1)
        for k, (c0, n) in enumerate(FF_CHUNKS)]

    if not decode:
        for piece in pieces:
            piece(lambda: None)
        return

    i = pl.program_id(0)
    n_groups = len(pieces)
    n_pages = pt_ref.shape[1]
    per_group = n_pages // n_groups
    assert n_pages % n_groups == 0 and n_groups % 2 == 0 and kbuf.shape[1] == per_group

    def page_copy(which, slot, p, page):
        hbm, buf = ((ck_hbm, kbuf), (cv_hbm, vbuf))[which]
        return pltpu.make_async_copy(hbm.at[layer, page], buf.at[slot, p], sem.at[which, slot])

    def start_group(row, g, slot):
        for p in range(per_group):
            page = pt_ref[row, g * per_group + p]
            page_copy(0, slot, p, page).start()
            page_copy(1, slot, p, page).start()

    def wait_group(slot):
        for p in range(per_group):
            page_copy(0, slot, p, 0).wait()
            page_copy(1, slot, p, 0).wait()

    @pl.when(i == 0)
    def _():
        start_group(0, 0, 0)

    qm = _decode_query(q_ref)
    q8 = _decode_query_rows(q_ref)
    state = (jnp.full((N_SCORE_ROWS, 1), NEG, F32), jnp.zeros((N_SCORE_ROWS, 1), F32),
             jnp.zeros((N_SCORE_ROWS, DA_V_DIM), F32))
    for g, piece in enumerate(pieces):
        slot = g % 2
        if g + 1 < n_groups:
            start_group(i, g + 1, 1 - slot)
        else:
            @pl.when(i + 1 < pl.num_programs(0))
            def _():
                start_group(i + 1, 0, 1 - slot)
        wait_group(slot)
        state, probs = _decode_scores(q8, [kbuf[slot, p] for p in range(per_group)], state)

        def add_values(state=state, probs=probs, slot=slot):
            vals["dec"] = _decode_values(state, probs, [vbuf[slot, p] for p in range(per_group)])

        piece(add_values)
        state = vals["dec"]
    lam = _lambda_vec(lq1_ref, lk1_ref, lq2_ref, lk2_ref, lam_init)
    _decode_finish(qm, state, kn_ref, vn_ref, lam, sg_ref, das_ref, lam_init)


def _finish(x, gm, da, w_o, norm_g, w_ffn_in, w_ffn_out, *, layer, tm, decode=None):
    n = x.shape[0]
    assert n % tm == 0
    steps = n // tm
    row = lambda width: pl.BlockSpec((tm, width), lambda i, *_: (i, 0))
    const = lambda shape, first: pl.BlockSpec(
        (None,) + shape, lambda i, *_: (first,) + (0,) * len(shape), pipeline_mode=pl.Buffered(1))
    gain = lambda k: const((1, D_MODEL), 4 * layer + k)
    in_specs = [row(D_MODEL), row(GM_WIDTH), row(DA_WIDTH), const((D_MODEL, D_MODEL), layer),
                gain(1), gain(2), gain(3), const((D_MODEL, 2 * D_FF), layer),
                const((D_FF, D_MODEL), layer)]
    args = [x, gm, da, w_o, norm_g, norm_g, norm_g, w_ffn_in, w_ffn_out]
    x_out = jax.ShapeDtypeStruct((n, D_MODEL), F32)
    if decode is None:
        return pl.pallas_call(
            functools.partial(_finish_kernel, decode=False, layer=layer, lam_init=None),
            grid=(steps,), in_specs=in_specs, out_specs=row(D_MODEL), out_shape=x_out,
            compiler_params=_params(1), name="finish_rows%d" % tm,
        )(*args)

    page_table, q, k_new, v_new, cache_k, cache_v, lams, sg, lam_init = decode
    bs, n_pages = page_table.shape
    assert bs == steps, "one sample row per grid step"
    n_groups = 1 + len(FF_CHUNKS)
    per_group = n_pages // n_groups
    rowspec = pl.BlockSpec((None, 1, DA_WIDTH), lambda i, *_: (i, 0, 0))
    in_specs += [rowspec, rowspec, rowspec] + [const((1, DA_QK_DIM), layer)] * 4 + [
        const((1, DA_V_DIM), layer), pl.BlockSpec(memory_space=pl.ANY),
        pl.BlockSpec(memory_space=pl.ANY)]
    args += [q.reshape(bs, 1, DA_WIDTH), k_new.reshape(bs, 1, DA_WIDTH),
             v_new.reshape(bs, 1, DA_WIDTH), *lams, sg, cache_k, cache_v]
    grid_spec = pltpu.PrefetchScalarGridSpec(
        num_scalar_prefetch=1, grid=(steps,), in_specs=in_specs,
        out_specs=[row(D_MODEL), rowspec],
        scratch_shapes=[pltpu.VMEM((2, per_group, PAGE_ROWS, DA_V_DIM), F32),
                        pltpu.VMEM((2, per_group, PAGE_ROWS, DA_V_DIM), F32),
                        pltpu.SemaphoreType.DMA((2, 2))],
    )
    x_new, da_s = pl.pallas_call(
        functools.partial(_finish_kernel, decode=True, layer=layer, lam_init=lam_init),
        grid_spec=grid_spec,
        out_shape=[x_out, jax.ShapeDtypeStruct((bs, 1, DA_WIDTH), BF16)],
        compiler_params=_params(1), name="finish_decode",
    )(page_table, *args)
    return x_new, da_s.reshape(bs, DA_WIDTH)


def _rope_tables(pos):
    half = ROT_DIM // 2
    inv_freq = ROPE_THETA ** (-jnp.arange(0, ROT_DIM, 2, dtype=F32) / ROT_DIM)
    ang = pos.astype(F32)[:, None] * inv_freq[None, :]
    cos, sin = jnp.cos(ang), jnp.sin(ang)
    npos = pos.shape[0]
    rest = DA_QK_DIM - ROT_DIM
    ones = jnp.ones((npos, rest), F32)
    zeros = jnp.zeros((npos, rest), F32)
    zh = jnp.zeros((npos, half), F32)
    c64 = jnp.concatenate([cos, cos, ones], axis=1)
    a64 = jnp.concatenate([-sin, zh, zeros], axis=1)
    b64 = jnp.concatenate([zh, sin, zeros], axis=1)
    two = lambda t: jnp.concatenate([t, t], axis=1)
    return two(c64), two(a64), two(b64)


def kernel(x_prompt, x_sample, cache_k, cache_v, page_table, w_in, w_gmlp_s, b_gmlp_s, ln_v_g, ln_v_b,
           lambda_q1, lambda_k1, lambda_q2, lambda_k2, subln_g, w_o, norm_g, w_ffn_in, w_ffn_out):
    batch, seq, _ = x_prompt.shape
    bs = x_sample.shape[0]
    n_pool = cache_k.shape[1]
    xp = x_prompt.reshape(batch * seq, D_MODEL)
    xs = x_sample.reshape(bs, D_MODEL)
    ck = cache_k.reshape(DEPTH, n_pool, PAGE_ROWS, DA_V_DIM)
    cv = cache_v.reshape(DEPTH, n_pool, PAGE_ROWS, DA_V_DIM)
    tabs_p = _rope_tables(jnp.arange(SEQ, dtype=jnp.int32))
    tabs_s = _rope_tables(jnp.full((1,), PAST_LEN, dtype=jnp.int32))
    w_in_b = w_in.astype(BF16)
    w_o_b = w_o.astype(BF16)
    w_fi_b = w_ffn_in.astype(BF16)
    w_fo_b = w_ffn_out.astype(BF16)
    gains = norm_g.reshape(DEPTH * 4, 1, D_MODEL)
    lng = ln_v_g.reshape(DEPTH, 1, GM_WIDTH)
    lnb = ln_v_b.reshape(DEPTH, 1, GM_WIDTH)
    bst = jnp.swapaxes(b_gmlp_s, 1, 2)
    lams = [t.reshape(DEPTH, 1, DA_QK_DIM) for t in (lambda_q1, lambda_k1, lambda_q2, lambda_k2)]
    sg = subln_g.reshape(DEPTH, 1, DA_V_DIM)

    kv_p = None
    k_s, v_s, gv_s = [], [], []
    for l in range(DEPTH):
        lam_init = _lambda_init(l)
        gm, q, kb, vb, k_all, v_all = _mixer_in(xp, gains, w_in_b, lng, lnb, w_gmlp_s, bst, *tabs_p,
                                                layer=l, sample=False, kv_prev=kv_p)
        kv_p = (k_all, v_all)
        da = _prompt_attn(q, kb, vb, *lams, sg, layer=l, lam_init=lam_init)
        gm_s, q_s, k, v, vg = _mixer_in(xs, gains, w_in_b, lng, lnb, w_gmlp_s, bst, *tabs_s,
                                        layer=l, sample=True)
        xp, da_s = _finish(xp, gm, da, w_o_b, gains, w_fi_b, w_fo_b, layer=l, tm=ROW_TILE,
                           decode=(page_table, q_s, k, v, ck, cv, lams, sg, lam_init))
        xs = _finish(xs, gm_s, da_s, w_o_b, gains, w_fi_b, w_fo_b, layer=l, tm=bs)
        k_s.append(k.reshape(bs, 1, DA_HEADS, 2 * DA_QK_DIM))
        v_s.append(v.reshape(bs, 1, DA_HEADS, DA_V_DIM))
        gv_s.append(vg.reshape(bs, 1, GM_WIDTH))

    k_prompt = kv_p[0].reshape(DEPTH, batch, seq, DA_HEADS, 2 * DA_QK_DIM)
    v_prompt = kv_p[1].reshape(DEPTH, batch, seq, DA_HEADS, DA_V_DIM)
    return (xp.reshape(batch, seq, D_MODEL), xs.reshape(bs, 1, D_MODEL),
            k_prompt, v_prompt, jnp.stack(k_s), jnp.stack(v_s), jnp.stack(gv_s))
```

```python
import functools
import math

import jax
import jax.numpy as jnp
from jax import lax
from jax.experimental import pallas as pl
from jax.experimental.pallas import tpu as pltpu

D_MODEL = 1024
SEQ = 2048
DEPTH = 4
PAST_LEN = 8192
PAGE_SIZE = 128
GM_WIDTH = 512
GM_HEADS = 4
GM_HEAD_DIM = 128
CHUNK = 128
DA_WIDTH = 512
DA_HEADS = 4
DA_V_DIM = 128
DA_QK_DIM = 64
ROT_DIM = 16
ROPE_THETA = 500000.0
D_FF = 2816
IN_COLS = 2 * GM_WIDTH + 3 * DA_WIDTH
EPS = 1e-6

LANES = 128
MXU_DIM = 256
VMEM_LIMIT_BYTES = 56 * 1024 * 1024

ROW_TILE = 512
Q_TILE = 256
FF_CHUNKS = ((0, 512), (512, 768), (1280, 768), (2048, 768))
PAGE_ROWS = PAGE_SIZE * DA_HEADS

Q_SCALE = (DA_QK_DIM ** -0.5) * math.log2(math.e)

F32 = jnp.float32
BF16 = jnp.bfloat16


def _lambda_init(layer_idx):
    return 0.8 - 0.6 * math.exp(-0.3 * layer_idx)


def _rms(x, g):
    return (x * lax.rsqrt(jnp.mean(x * x, axis=-1, keepdims=True) + EPS)) * g


def _dot(a, b):
    return jnp.dot(a, b, preferred_element_type=F32)


def _dot_nt(a, b):
    return lax.dot_general(a, b, (((1,), (1,)), ((), ())), preferred_element_type=F32)


def _lambda_vec(lq1_ref, lk1_ref, lq2_ref, lk2_ref, lam_init):
    a = jnp.sum(lq1_ref[...] * lk1_ref[...], axis=-1, keepdims=True)
    b = jnp.sum(lq2_ref[...] * lk2_ref[...], axis=-1, keepdims=True)
    return jnp.exp(a) - jnp.exp(b) + lam_init


def _layer_spec(shape, layer, grid_rank):
    nd = len(shape)
    if grid_rank == 1:
        index_map = lambda i: (layer,) + (0,) * nd
    else:
        index_map = lambda i, j: (layer,) + (0,) * nd
    return pl.BlockSpec((None,) + tuple(shape), index_map, pipeline_mode=pl.Buffered(1))


def _params(grid_rank):
    return pltpu.CompilerParams(dimension_semantics=("arbitrary",) * grid_rank,
                                vmem_limit_bytes=VMEM_LIMIT_BYTES)


def _mixer_in_kernel(*refs, sample, n_alias, layer):
    (x_ref, g_ref, w_ref, lng_ref, lnb_ref, ws_ref, bst_ref, cos_ref, sa_ref, sb_ref) = refs[:10]
    out_refs = refs[10 + n_alias:]
    if sample:
        gm_ref, q_ref, k_ref, v_ref, vg_ref = out_refs
    else:
        gm_ref, q_ref, kb_ref, vb_ref, k_ref, v_ref = out_refs
        k_out, v_out = (k_ref, v_ref) if n_alias else (k_ref.at[layer], v_ref.at[layer])
    rows = x_ref.shape[0]
    xn = _rms(x_ref[...], g_ref[...]).astype(BF16)

    def proj(c0, n):
        return _dot(xn, w_ref[:, c0:c0 + n])

    o = 2 * GM_WIDTH
    u_raw = proj(0, GM_WIDTH)
    vr_raw = proj(GM_WIDTH, GM_WIDTH)
    qf = proj(o, DA_WIDTH)
    kf = proj(o + DA_WIDTH, DA_WIDTH)
    vf = proj(o + 2 * DA_WIDTH, DA_WIDTH)
    u = jax.nn.gelu(u_raw)
    vr = jax.nn.gelu(vr_raw)
    mu = jnp.mean(vr, axis=-1, keepdims=True)
    var = jnp.mean((vr - mu) ** 2, axis=-1, keepdims=True)
    vg = (vr - mu) * lax.rsqrt(var + EPS) * lng_ref[...] + lnb_ref[...]

    if sample:
        vg_ref[...] = vg
        for h in range(GM_HEADS):
            hs = slice(h * GM_HEAD_DIM, (h + 1) * GM_HEAD_DIM)
            w00 = ws_ref[h, 0:1, 0:1]
            b00 = bst_ref[0:1, h:h + 1]
            mixed = vg[:, hs] * w00 + b00
            gm_ref[:, hs] = (u[:, hs] * mixed).astype(BF16)
    else:
        r = lax.broadcasted_iota(jnp.int32, (CHUNK, CHUNK), 0)
        c = lax.broadcasted_iota(jnp.int32, (CHUNK, CHUNK), 1)
        tril = r >= c
        vgb = vg.astype(BF16)
        for h in range(GM_HEADS):
            hs = slice(h * GM_HEAD_DIM, (h + 1) * GM_HEAD_DIM)
            wsm = jnp.where(tril, ws_ref[h], 0.0).astype(BF16)
            bcol = bst_ref[:, h:h + 1]
            for ci in range(rows // CHUNK):
                rs = slice(ci * CHUNK, (ci + 1) * CHUNK)
                mixed = _dot(wsm, vgb[rs, hs]) + bcol
                gm_ref[rs, hs] = (u[rs, hs] * mixed).astype(BF16)

    cos = cos_ref[...]
    sa = sa_ref[...]
    sb = sb_ref[...]

    def rope(t):
        return (t * cos + pltpu.roll(t, LANES - ROT_DIM // 2, 1) * sa
                + pltpu.roll(t, ROT_DIM // 2, 1) * sb)

    for h in range(DA_HEADS):
        hs = slice(h * DA_V_DIM, (h + 1) * DA_V_DIM)
        q_ref[:, hs] = (rope(qf[:, hs]) * Q_SCALE).astype(q_ref.dtype)
    for h in range(DA_HEADS):
        hs = slice(h * DA_V_DIM, (h + 1) * DA_V_DIM)
        kr = rope(kf[:, hs])
        if sample:
            k_ref[:, hs] = kr
        else:
            k_out[pl.ds(h, rows, stride=DA_HEADS), :] = kr
            v_out[pl.ds(h, rows, stride=DA_HEADS), :] = vf[:, hs]
            kb_ref[:, hs] = kr.astype(BF16)
    if sample:
        v_ref[...] = vf
    else:
        vb_ref[...] = vf.astype(BF16)
        if n_alias == 0:
            for other in range(DEPTH):
                if other != layer:
                    k_ref[other] = jnp.zeros(k_ref.shape[1:], F32)
                    v_ref[other] = jnp.zeros(v_ref.shape[1:], F32)


def _mixer_in(x, norm_g, w_in, lng, lnb, ws, bst, cos, sa, sb, *, layer, sample, kv_prev=None):
    n = x.shape[0]
    tm = n if sample else ROW_TILE
    assert n % tm == 0 and (sample or SEQ % tm == 0)
    steps = n // tm
    tab_steps = 1 if sample else SEQ // tm
    row = lambda width: pl.BlockSpec((tm, width), lambda i: (i, 0))
    tab = pl.BlockSpec((cos.shape[0] // tab_steps, LANES), lambda i: (i % tab_steps, 0))
    in_specs = [
        row(D_MODEL),
        pl.BlockSpec((None, 1, D_MODEL), lambda i: (4 * layer, 0, 0), pipeline_mode=pl.Buffered(1)),
        _layer_spec((D_MODEL, IN_COLS), layer, 1),
        _layer_spec((1, GM_WIDTH), layer, 1), _layer_spec((1, GM_WIDTH), layer, 1),
        _layer_spec((GM_HEADS, CHUNK, CHUNK), layer, 1), _layer_spec((CHUNK, GM_HEADS), layer, 1),
        tab, tab, tab,
    ]
    args = [x, norm_g, w_in, lng, lnb, ws, bst, cos, sa, sb]
    sds = jax.ShapeDtypeStruct
    aliases = {}
    if sample:
        out_shape = [sds((n, GM_WIDTH), BF16), sds((n, DA_WIDTH), F32), sds((n, DA_WIDTH), F32),
                     sds((n, DA_WIDTH), F32), sds((n, GM_WIDTH), F32)]
        out_specs = [row(s.shape[1]) for s in out_shape]
    else:
        small = [sds((n, GM_WIDTH), BF16), sds((n, DA_WIDTH), BF16), sds((n, DA_WIDTH), BF16),
                 sds((n, DA_WIDTH), BF16)]
        big = sds((DEPTH, n * DA_HEADS, DA_V_DIM), F32)
        out_shape = small + [big, big]
        if kv_prev is None:
            big_spec = pl.BlockSpec((DEPTH, tm * DA_HEADS, DA_V_DIM), lambda i: (0, i, 0))
        else:
            big_spec = pl.BlockSpec((None, tm * DA_HEADS, DA_V_DIM), lambda i: (layer, i, 0))
            for t in kv_prev:
                aliases[len(args)] = len(small) + len(aliases)
                args.append(t)
                in_specs.append(pl.BlockSpec(memory_space=pl.ANY))
        out_specs = [row(s.shape[1]) for s in small] + [big_spec, big_spec]
    return pl.pallas_call(
        functools.partial(_mixer_in_kernel, sample=sample, n_alias=len(aliases), layer=layer),
        grid=(steps,), in_specs=in_specs, out_specs=out_specs, out_shape=out_shape,
        input_output_aliases=aliases, compiler_params=_params(1),
        name="mixer_in_sample" if sample else "mixer_in_prompt",
    )(*args)


def _prompt_attn_kernel(q_ref, k_ref, v_ref, lq1_ref, lk1_ref, lq2_ref, lk2_ref, sg_ref, o_ref,
                        vaug_ref, *, lam_init):
    lam = _lambda_vec(lq1_ref, lk1_ref, lq2_ref, lk2_ref, lam_init)
    tq = Q_TILE
    vaug_ref[:, 0:DA_V_DIM] = v_ref[...]
    vaug_ref[:, DA_V_DIM:] = jnp.ones((SEQ, MXU_DIM - DA_V_DIM), BF16)
    lane = lax.broadcasted_iota(jnp.int32, (tq, LANES), 1)
    first_half = lane < DA_QK_DIM
    rr = lax.broadcasted_iota(jnp.int32, (tq, tq), 0)
    cc = lax.broadcasted_iota(jnp.int32, (tq, tq), 1)
    causal = rr >= cc
    neg = jnp.finfo(F32).min
    zero = jnp.zeros((), BF16)
    for i in reversed(range(SEQ // tq)):
        past = i * tq
        qi = q_ref[past:past + tq, :]
        outs = []
        for qc in (jnp.where(first_half, qi, zero), jnp.where(first_half, zero, qi)):
            s = _dot_nt(qc, k_ref[0:past + tq, :])
            sd = jnp.where(causal, s[:, past:], neg)
            m = jnp.max(sd, axis=-1, keepdims=True)
            if past:
                m = jnp.maximum(m, jnp.max(s[:, :past], axis=-1, keepdims=True))
            p = jnp.exp2(sd - m).astype(BF16)
            if past:
                p = jnp.concatenate([jnp.exp2(s[:, :past] - m).astype(BF16), p], axis=1)
            acc = _dot(p, vaug_ref[0:past + tq, :])
            outs.append(acc[:, 0:DA_V_DIM] / acc[:, DA_V_DIM:DA_V_DIM + 1])
        o = outs[0] - lam * outs[1]
        y = _rms(o, sg_ref[...]) * (1.0 - lam_init)
        o_ref[past:past + tq, :] = y.astype(o_ref.dtype)


def _prompt_attn(q, kb, vb, lq1, lk1, lq2, lk2, sg, *, layer, lam_init):
    n = q.shape[0]
    batch = n // SEQ
    head = pl.BlockSpec((SEQ, DA_V_DIM), lambda b, h: (b, h))
    vec = _layer_spec((1, DA_QK_DIM), layer, 2)
    return pl.pallas_call(
        functools.partial(_prompt_attn_kernel, lam_init=lam_init),
        grid=(batch, DA_HEADS),
        in_specs=[head, head, head, vec, vec, vec, vec, _layer_spec((1, DA_V_DIM), layer, 2)],
        out_specs=head,
        out_shape=jax.ShapeDtypeStruct((n, DA_WIDTH), BF16),
        scratch_shapes=[pltpu.VMEM((SEQ, MXU_DIM), BF16)],
        compiler_params=_params(2),
        name="prompt_attn",
    )(q, kb, vb, lq1, lk1, lq2, lk2, sg)


N_SCORE_ROWS = 2 * DA_HEADS
NEG = float(jnp.finfo(jnp.float32).min)


def _per_row_head(vec_ref):
    rid = lax.broadcasted_iota(jnp.int32, (N_SCORE_ROWS, LANES), 0)
    row_head = lax.shift_right_logical(rid, 1)
    out = jnp.zeros((N_SCORE_ROWS, LANES), F32)
    for h in range(DA_HEADS):
        hs = slice(h * LANES, (h + 1) * LANES)
        out = jnp.where(row_head == h, jnp.broadcast_to(vec_ref[:, hs], (N_SCORE_ROWS, LANES)), out)
    return out


def _decode_query(q_ref):
    rid = lax.broadcasted_iota(jnp.int32, (N_SCORE_ROWS, LANES), 0)
    lid = lax.broadcasted_iota(jnp.int32, (N_SCORE_ROWS, LANES), 1)
    own_half = lax.shift_right_logical(lid, int(math.log2(DA_QK_DIM))) == (rid & 1)
    return jnp.where(own_half, _per_row_head(q_ref), 0.0)


def _decode_query_rows(q_ref):
    rid = lax.broadcasted_iota(jnp.int32, (8, LANES), 0)
    out = jnp.zeros((8, LANES), F32)
    for h in range(DA_HEADS):
        hs = slice(h * LANES, (h + 1) * LANES)
        out = jnp.where((rid & (DA_HEADS - 1)) == h, jnp.broadcast_to(q_ref[:, hs], (8, LANES)), out)
    return out


def _decode_scores(q8, k_pages, state):
    cid = lax.broadcasted_iota(jnp.int32, (N_SCORE_ROWS, PAGE_ROWS), 1)
    rid = lax.broadcasted_iota(jnp.int32, (N_SCORE_ROWS, PAGE_ROWS), 0)
    own_col = (cid & (DA_HEADS - 1)) == lax.shift_right_logical(rid, 1)
    first_half = (rid & 1) == 0
    sub = q8.shape[0]

    def scores(kp):
        prod = (kp.reshape(PAGE_ROWS // sub, sub, LANES) * q8[None]).reshape(PAGE_ROWS, LANES)
        t = prod.T
        s1 = jnp.sum(t[0:DA_QK_DIM], axis=0, keepdims=True)
        s2 = jnp.sum(t[DA_QK_DIM:], axis=0, keepdims=True)
        return jnp.where(own_col, jnp.where(first_half, s1, s2), NEG)

    m_old, l_old, acc_old = state
    s = [scores(kp) for kp in k_pages]
    m_new = m_old
    for sp in s:
        m_new = jnp.maximum(m_new, jnp.max(sp, axis=-1, keepdims=True))
    alpha = jnp.exp2(m_old - m_new)
    l_new = alpha * l_old
    probs = []
    for sp in s:
        pr = jnp.exp2(sp - m_new)
        l_new = l_new + jnp.sum(pr, axis=-1, keepdims=True)
        probs.append(pr.astype(BF16))
    return (m_new, l_new, alpha * acc_old), probs


def _decode_values(state, probs, v_pages):
    m_new, l_new, acc = state
    for pr, vp in zip(probs, v_pages):
        acc = acc + _dot(pr, vp.astype(BF16))
    return m_new, l_new, acc


def _decode_finish(qm, state, kn_ref, vn_ref, lam, sg_ref, o_ref, lam_init):
    m_old, l_old, acc = state
    s_new = jnp.sum(qm * _per_row_head(kn_ref), axis=-1, keepdims=True)
    m_f = jnp.maximum(m_old, s_new)
    a = jnp.exp2(m_old - m_f)
    p_new = jnp.exp2(s_new - m_f)
    l_f = a * l_old + p_new
    full = (a * acc + p_new * _per_row_head(vn_ref)) / l_f
    for h in range(DA_HEADS):
        hs = slice(h * DA_V_DIM, (h + 1) * DA_V_DIM)
        o = full[2 * h:2 * h + 1, :] - lam * full[2 * h + 1:2 * h + 2, :]
        o_ref[:, hs] = (_rms(o, sg_ref[...]) * (1.0 - lam_init)).astype(o_ref.dtype)


def _finish_kernel(*refs, decode, layer, lam_init):
    if decode:
        pt_ref, refs = refs[0], refs[1:]
        (q_ref, kn_ref, vn_ref, lq1_ref, lk1_ref, lq2_ref, lk2_ref, sg_ref, ck_hbm, cv_hbm,
         o_ref, das_ref, kbuf, vbuf, sem) = refs[9:]
    else:
        (o_ref,) = refs[9:]
    x_ref, gm_ref, da_ref, wo_ref, g1_ref, g2_ref, g3_ref, wi_ref, wf_ref = refs[:9]
    vals = {}

    def project_and_norm(between_matmuls):
        mix = _dot(gm_ref[...], wo_ref[0:GM_WIDTH, :]) + _dot(da_ref[...], wo_ref[GM_WIDTH:, :])
        between_matmuls()
        vals["x1"] = x_ref[...] + _rms(mix, g1_ref[...])
        vals["xn"] = _rms(vals["x1"], g2_ref[...]).astype(BF16)

    def ffn_chunk(c0, n, last, between_matmuls):
        gate = _dot(vals["xn"], wi_ref[:, c0:c0 + n])
        up = _dot(vals["xn"], wi_ref[:, D_FF + c0:D_FF + c0 + n])
        between_matmuls()
        act = (jax.nn.silu(gate) * up).astype(BF16)
        part = _dot(act, wf_ref[c0:c0 + n, :])
        vals["f"] = part if c0 == 0 else vals["f"] + part
        if last:
            o_ref[...] = vals["x1"] + _rms(vals["f"], g3_ref[...])

    chunk_pieces = [functools.partial(ffn_chunk, c0, n, k == len(FF_CHUNKS) - 1)
                    for k, (c0, n) in enumerate(FF_CHUNKS)]

    def first_piece(between_matmuls):
        project_and_norm(between_matmuls)
        chunk_pieces[0](lambda: None)

    pieces = [first_piece] + chunk_pieces[1:]

    if not decode:
        for piece in pieces:
            piece(lambda: None)
        return

    i = pl.program_id(0)
    n_groups = len(pieces)
    n_pages = pt_ref.shape[1]
    per_group = n_pages // n_groups
    assert n_pages % n_groups == 0 and n_groups % 2 == 0 and kbuf.shape[1] == per_group

    def page_copy(which, slot, p, page):
        hbm, buf = ((ck_hbm, kbuf), (cv_hbm, vbuf))[which]
        return pltpu.make_async_copy(hbm.at[layer, page], buf.at[slot, p], sem.at[which, slot])

    def start_group(row, g, slot):
        for p in range(per_group):
            page = pt_ref[row, g * per_group + p]
            page_copy(0, slot, p, page).start()
            page_copy(1, slot, p, page).start()

    def wait_group(slot):
        for p in range(per_group):
            page_copy(0, slot, p, 0).wait()
            page_copy(1, slot, p, 0).wait()

    @pl.when(i == 0)
    def _():
        start_group(0, 0, 0)

    qm = _decode_query(q_ref)
    q8 = _decode_query_rows(q_ref)
    state = (jnp.full((N_SCORE_ROWS, 1), NEG, F32), jnp.zeros((N_SCORE_ROWS, 1), F32),
             jnp.zeros((N_SCORE_ROWS, DA_V_DIM), F32))
    for g, piece in enumerate(pieces):
        slot = g % 2
        if g + 1 < n_groups:
            start_group(i, g + 1, 1 - slot)
        else:
            @pl.when(i + 1 < pl.num_programs(0))
            def _():
                start_group(i + 1, 0, 1 - slot)
        wait_group(slot)
        state, probs = _decode_scores(q8, [kbuf[slot, p] for p in range(per_group)], state)

        def add_values(state=state, probs=probs, slot=slot):
            vals["dec"] = _decode_values(state, probs, [vbuf[slot, p] for p in range(per_group)])

        piece(add_values)
        state = vals["dec"]
    lam = _lambda_vec(lq1_ref, lk1_ref, lq2_ref, lk2_ref, lam_init)
    _decode_finish(qm, state, kn_ref, vn_ref, lam, sg_ref, das_ref, lam_init)


def _finish(x, gm, da, w_o, norm_g, w_ffn_in, w_ffn_out, *, layer, tm, decode=None):
    n = x.shape[0]
    assert n % tm == 0
    steps = n // tm
    row = lambda width: pl.BlockSpec((tm, width), lambda i, *_: (i, 0))
    const = lambda shape, first: pl.BlockSpec(
        (None,) + shape, lambda i, *_: (first,) + (0,) * len(shape), pipeline_mode=pl.Buffered(1))
    gain = lambda k: const((1, D_MODEL), 4 * layer + k)
    in_specs = [row(D_MODEL), row(GM_WIDTH), row(DA_WIDTH), const((D_MODEL, D_MODEL), layer),
                gain(1), gain(2), gain(3), const((D_MODEL, 2 * D_FF), layer),
                const((D_FF, D_MODEL), layer)]
    args = [x, gm, da, w_o, norm_g, norm_g, norm_g, w_ffn_in, w_ffn_out]
    x_out = jax.ShapeDtypeStruct((n, D_MODEL), F32)
    if decode is None:
        return pl.pallas_call(
            functools.partial(_finish_kernel, decode=False, layer=layer, lam_init=None),
            grid=(steps,), in_specs=in_specs, out_specs=row(D_MODEL), out_shape=x_out,
            compiler_params=_params(1), name="finish_rows%d" % tm,
        )(*args)

    page_table, q, k_new, v_new, cache_k, cache_v, lams, sg, lam_init = decode
    bs, n_pages = page_table.shape
    assert bs == steps, "one sample row per grid step"
    n_groups = len(FF_CHUNKS)
    per_group = n_pages // n_groups
    rowspec = pl.BlockSpec((None, 1, DA_WIDTH), lambda i, *_: (i, 0, 0))
    in_specs += [rowspec, rowspec, rowspec] + [const((1, DA_QK_DIM), layer)] * 4 + [
        const((1, DA_V_DIM), layer), pl.BlockSpec(memory_space=pl.ANY),
        pl.BlockSpec(memory_space=pl.ANY)]
    args += [q.reshape(bs, 1, DA_WIDTH), k_new.reshape(bs, 1, DA_WIDTH),
             v_new.reshape(bs, 1, DA_WIDTH), *lams, sg, cache_k, cache_v]
    grid_spec = pltpu.PrefetchScalarGridSpec(
        num_scalar_prefetch=1, grid=(steps,), in_specs=in_specs,
        out_specs=[row(D_MODEL), rowspec],
        scratch_shapes=[pltpu.VMEM((2, per_group, PAGE_ROWS, DA_V_DIM), F32),
                        pltpu.VMEM((2, per_group, PAGE_ROWS, DA_V_DIM), F32),
                        pltpu.SemaphoreType.DMA((2, 2))],
    )
    x_new, da_s = pl.pallas_call(
        functools.partial(_finish_kernel, decode=True, layer=layer, lam_init=lam_init),
        grid_spec=grid_spec,
        out_shape=[x_out, jax.ShapeDtypeStruct((bs, 1, DA_WIDTH), BF16)],
        compiler_params=_params(1), name="finish_decode",
    )(page_table, *args)
    return x_new, da_s.reshape(bs, DA_WIDTH)


def _rope_tables(pos):
    half = ROT_DIM // 2
    inv_freq = ROPE_THETA ** (-jnp.arange(0, ROT_DIM, 2, dtype=F32) / ROT_DIM)
    ang = pos.astype(F32)[:, None] * inv_freq[None, :]
    cos, sin = jnp.cos(ang), jnp.sin(ang)
    npos = pos.shape[0]
    rest = DA_QK_DIM - ROT_DIM
    ones = jnp.ones((npos, rest), F32)
    zeros = jnp.zeros((npos, rest), F32)
    zh = jnp.zeros((npos, half), F32)
    c64 = jnp.concatenate([cos, cos, ones], axis=1)
    a64 = jnp.concatenate([-sin, zh, zeros], axis=1)
    b64 = jnp.concatenate([zh, sin, zeros], axis=1)
    two = lambda t: jnp.concatenate([t, t], axis=1)
    return two(c64), two(a64), two(b64)


def kernel(x_prompt, x_sample, cache_k, cache_v, page_table, w_in, w_gmlp_s, b_gmlp_s, ln_v_g, ln_v_b,
           lambda_q1, lambda_k1, lambda_q2, lambda_k2, subln_g, w_o, norm_g, w_ffn_in, w_ffn_out):
    batch, seq, _ = x_prompt.shape
    bs = x_sample.shape[0]
    n_pool = cache_k.shape[1]
    xp = x_prompt.reshape(batch * seq, D_MODEL)
    xs = x_sample.reshape(bs, D_MODEL)
    ck = cache_k.reshape(DEPTH, n_pool, PAGE_ROWS, DA_V_DIM)
    cv = cache_v.reshape(DEPTH, n_pool, PAGE_ROWS, DA_V_DIM)
    tabs_p = _rope_tables(jnp.arange(SEQ, dtype=jnp.int32))
    tabs_s = _rope_tables(jnp.full((1,), PAST_LEN, dtype=jnp.int32))
    w_in_b = w_in.astype(BF16)
    w_o_b = w_o.astype(BF16)
    w_fi_b = w_ffn_in.astype(BF16)
    w_fo_b = w_ffn_out.astype(BF16)
    gains = norm_g.reshape(DEPTH * 4, 1, D_MODEL)
    lng = ln_v_g.reshape(DEPTH, 1, GM_WIDTH)
    lnb = ln_v_b.reshape(DEPTH, 1, GM_WIDTH)
    bst = jnp.swapaxes(b_gmlp_s, 1, 2)
    lams = [t.reshape(DEPTH, 1, DA_QK_DIM) for t in (lambda_q1, lambda_k1, lambda_q2, lambda_k2)]
    sg = subln_g.reshape(DEPTH, 1, DA_V_DIM)

    kv_p = None
    k_s, v_s, gv_s = [], [], []
    for l in range(DEPTH):
        lam_init = _lambda_init(l)
        gm, q, kb, vb, k_all, v_all = _mixer_in(xp, gains, w_in_b, lng, lnb, w_gmlp_s, bst, *tabs_p,
                                                layer=l, sample=False, kv_prev=kv_p)
        kv_p = (k_all, v_all)
        da = _prompt_attn(q, kb, vb, *lams, sg, layer=l, lam_init=lam_init)
        gm_s, q_s, k, v, vg = _mixer_in(xs, gains, w_in_b, lng, lnb, w_gmlp_s, bst, *tabs_s,
                                        layer=l, sample=True)
        xp, da_s = _finish(xp, gm, da, w_o_b, gains, w_fi_b, w_fo_b, layer=l, tm=ROW_TILE,
                           decode=(page_table, q_s, k, v, ck, cv, lams, sg, lam_init))
        xs = _finish(xs, gm_s, da_s, w_o_b, gains, w_fi_b, w_fo_b, layer=l, tm=bs)
        k_s.append(k.reshape(bs, 1, DA_HEADS, 2 * DA_QK_DIM))
        v_s.append(v.reshape(bs, 1, DA_HEADS, DA_V_DIM))
        gv_s.append(vg.reshape(bs, 1, GM_WIDTH))

    k_prompt = kv_p[0].reshape(DEPTH, batch, seq, DA_HEADS, 2 * DA_QK_DIM)
    v_prompt = kv_p[1].reshape(DEPTH, batch, seq, DA_HEADS, DA_V_DIM)
    return (xp.reshape(batch, seq, D_MODEL), xs.reshape(bs, 1, D_MODEL),
            k_prompt, v_prompt, jnp.stack(k_s), jnp.stack(v_s), jnp.stack(gv_s))
```

```python
import functools
import math

import jax
import jax.numpy as jnp
from jax import lax
from jax.experimental import pallas as pl
from jax.experimental.pallas import tpu as pltpu

D_MODEL = 1024
SEQ = 2048
DEPTH = 4
PAST_LEN = 8192
PAGE_SIZE = 128
GM_WIDTH = 512
GM_HEADS = 4
GM_HEAD_DIM = 128
CHUNK = 128
DA_WIDTH = 512
DA_HEADS = 4
DA_V_DIM = 128
DA_QK_DIM = 64
ROT_DIM = 16
ROPE_THETA = 500000.0
D_FF = 2816
IN_COLS = 2 * GM_WIDTH + 3 * DA_WIDTH
EPS = 1e-6

LANES = 128
MXU_DIM = 256
VMEM_LIMIT_BYTES = 56 * 1024 * 1024

ROW_TILE = 512
Q_TILE = 256
FF_CHUNKS = ((0, 512), (512, 768), (1280, 768), (2048, 768))
PAGE_ROWS = PAGE_SIZE * DA_HEADS
RING_SLOTS = 3
RING_VMEM_LIMIT_BYTES = 61 * 1024 * 1024

Q_SCALE = (DA_QK_DIM ** -0.5) * math.log2(math.e)

F32 = jnp.float32
BF16 = jnp.bfloat16


def _lambda_init(layer_idx):
    return 0.8 - 0.6 * math.exp(-0.3 * layer_idx)


def _rms(x, g):
    return (x * lax.rsqrt(jnp.mean(x * x, axis=-1, keepdims=True) + EPS)) * g


def _dot(a, b):
    return jnp.dot(a, b, preferred_element_type=F32)


def _dot_nt(a, b):
    return lax.dot_general(a, b, (((1,), (1,)), ((), ())), preferred_element_type=F32)


def _lambda_vec(lq1_ref, lk1_ref, lq2_ref, lk2_ref, lam_init):
    a = jnp.sum(lq1_ref[...] * lk1_ref[...], axis=-1, keepdims=True)
    b = jnp.sum(lq2_ref[...] * lk2_ref[...], axis=-1, keepdims=True)
    return jnp.exp(a) - jnp.exp(b) + lam_init


def _layer_spec(shape, layer, grid_rank):
    nd = len(shape)
    if grid_rank == 1:
        index_map = lambda i: (layer,) + (0,) * nd
    else:
        index_map = lambda i, j: (layer,) + (0,) * nd
    return pl.BlockSpec((None,) + tuple(shape), index_map, pipeline_mode=pl.Buffered(1))


def _params(grid_rank):
    return pltpu.CompilerParams(dimension_semantics=("arbitrary",) * grid_rank,
                                vmem_limit_bytes=VMEM_LIMIT_BYTES)


def _mixer_in_kernel(*refs, sample, n_alias, layer):
    (x_ref, g_ref, w_ref, lng_ref, lnb_ref, ws_ref, bst_ref, cos_ref, sa_ref, sb_ref) = refs[:10]
    out_refs = refs[10 + n_alias:]
    if sample:
        gm_ref, q_ref, k_ref, v_ref, vg_ref = out_refs
    else:
        gm_ref, q_ref, kb_ref, vb_ref, k_ref, v_ref = out_refs
        k_out, v_out = (k_ref, v_ref) if n_alias else (k_ref.at[layer], v_ref.at[layer])
    rows = x_ref.shape[0]
    xn = _rms(x_ref[...], g_ref[...]).astype(BF16)

    def proj(c0, n):
        return _dot(xn, w_ref[:, c0:c0 + n])

    o = 2 * GM_WIDTH
    u_raw = proj(0, GM_WIDTH)
    vr_raw = proj(GM_WIDTH, GM_WIDTH)
    qf = proj(o, DA_WIDTH)
    kf = proj(o + DA_WIDTH, DA_WIDTH)
    vf = proj(o + 2 * DA_WIDTH, DA_WIDTH)
    u = jax.nn.gelu(u_raw)
    vr = jax.nn.gelu(vr_raw)
    mu = jnp.mean(vr, axis=-1, keepdims=True)
    var = jnp.mean((vr - mu) ** 2, axis=-1, keepdims=True)
    vg = (vr - mu) * lax.rsqrt(var + EPS) * lng_ref[...] + lnb_ref[...]

    if sample:
        vg_ref[...] = vg
        for h in range(GM_HEADS):
            hs = slice(h * GM_HEAD_DIM, (h + 1) * GM_HEAD_DIM)
            w00 = ws_ref[h, 0:1, 0:1]
            b00 = bst_ref[0:1, h:h + 1]
            mixed = vg[:, hs] * w00 + b00
            gm_ref[:, hs] = (u[:, hs] * mixed).astype(BF16)
    else:
        r = lax.broadcasted_iota(jnp.int32, (CHUNK, CHUNK), 0)
        c = lax.broadcasted_iota(jnp.int32, (CHUNK, CHUNK), 1)
        tril = r >= c
        vgb = vg.astype(BF16)
        for h in range(GM_HEADS):
            hs = slice(h * GM_HEAD_DIM, (h + 1) * GM_HEAD_DIM)
            wsm = jnp.where(tril, ws_ref[h], 0.0).astype(BF16)
            bcol = bst_ref[:, h:h + 1]
            for ci in range(rows // CHUNK):
                rs = slice(ci * CHUNK, (ci + 1) * CHUNK)
                mixed = _dot(wsm, vgb[rs, hs]) + bcol
                gm_ref[rs, hs] = (u[rs, hs] * mixed).astype(BF16)

    cos = cos_ref[...]
    sa = sa_ref[...]
    sb = sb_ref[...]

    def rope(t):
        return (t * cos + pltpu.roll(t, LANES - ROT_DIM // 2, 1) * sa
                + pltpu.roll(t, ROT_DIM // 2, 1) * sb)

    for h in range(DA_HEADS):
        hs = slice(h * DA_V_DIM, (h + 1) * DA_V_DIM)
        q_ref[:, hs] = (rope(qf[:, hs]) * Q_SCALE).astype(q_ref.dtype)
    for h in range(DA_HEADS):
        hs = slice(h * DA_V_DIM, (h + 1) * DA_V_DIM)
        kr = rope(kf[:, hs])
        if sample:
            k_ref[:, hs] = kr
        else:
            k_out[pl.ds(h, rows, stride=DA_HEADS), :] = kr
            v_out[pl.ds(h, rows, stride=DA_HEADS), :] = vf[:, hs]
            kb_ref[:, hs] = kr.astype(BF16)
    if sample:
        v_ref[...] = vf
    else:
        vb_ref[...] = vf.astype(BF16)
        if n_alias == 0:
            for other in range(DEPTH):
                if other != layer:
                    k_ref[other] = jnp.zeros(k_ref.shape[1:], F32)
                    v_ref[other] = jnp.zeros(v_ref.shape[1:], F32)


def _mixer_in(x, norm_g, w_in, lng, lnb, ws, bst, cos, sa, sb, *, layer, sample, kv_prev=None):
    n = x.shape[0]
    tm = n if sample else ROW_TILE
    assert n % tm == 0 and (sample or SEQ % tm == 0)
    steps = n // tm
    tab_steps = 1 if sample else SEQ // tm
    row = lambda width: pl.BlockSpec((tm, width), lambda i: (i, 0))
    tab = pl.BlockSpec((cos.shape[0] // tab_steps, LANES), lambda i: (i % tab_steps, 0))
    in_specs = [
        row(D_MODEL),
        pl.BlockSpec((None, 1, D_MODEL), lambda i: (4 * layer, 0, 0), pipeline_mode=pl.Buffered(1)),
        _layer_spec((D_MODEL, IN_COLS), layer, 1),
        _layer_spec((1, GM_WIDTH), layer, 1), _layer_spec((1, GM_WIDTH), layer, 1),
        _layer_spec((GM_HEADS, CHUNK, CHUNK), layer, 1), _layer_spec((CHUNK, GM_HEADS), layer, 1),
        tab, tab, tab,
    ]
    args = [x, norm_g, w_in, lng, lnb, ws, bst, cos, sa, sb]
    sds = jax.ShapeDtypeStruct
    aliases = {}
    if sample:
        out_shape = [sds((n, GM_WIDTH), BF16), sds((n, DA_WIDTH), F32), sds((n, DA_WIDTH), F32),
                     sds((n, DA_WIDTH), F32), sds((n, GM_WIDTH), F32)]
        out_specs = [row(s.shape[1]) for s in out_shape]
    else:
        small = [sds((n, GM_WIDTH), BF16), sds((n, DA_WIDTH), BF16), sds((n, DA_WIDTH), BF16),
                 sds((n, DA_WIDTH), BF16)]
        big = sds((DEPTH, n * DA_HEADS, DA_V_DIM), F32)
        out_shape = small + [big, big]
        if kv_prev is None:
            big_spec = pl.BlockSpec((DEPTH, tm * DA_HEADS, DA_V_DIM), lambda i: (0, i, 0))
        else:
            big_spec = pl.BlockSpec((None, tm * DA_HEADS, DA_V_DIM), lambda i: (layer, i, 0))
            for t in kv_prev:
                aliases[len(args)] = len(small) + len(aliases)
                args.append(t)
                in_specs.append(pl.BlockSpec(memory_space=pl.ANY))
        out_specs = [row(s.shape[1]) for s in small] + [big_spec, big_spec]
    return pl.pallas_call(
        functools.partial(_mixer_in_kernel, sample=sample, n_alias=len(aliases), layer=layer),
        grid=(steps,), in_specs=in_specs, out_specs=out_specs, out_shape=out_shape,
        input_output_aliases=aliases, compiler_params=_params(1),
        name="mixer_in_sample" if sample else "mixer_in_prompt",
    )(*args)


def _prompt_attn_kernel(q_ref, k_ref, v_ref, lq1_ref, lk1_ref, lq2_ref, lk2_ref, sg_ref, o_ref,
                        vaug_ref, *, lam_init):
    lam = _lambda_vec(lq1_ref, lk1_ref, lq2_ref, lk2_ref, lam_init)
    tq = Q_TILE
    vaug_ref[:, 0:DA_V_DIM] = v_ref[...]
    vaug_ref[:, DA_V_DIM:] = jnp.ones((SEQ, MXU_DIM - DA_V_DIM), BF16)
    lane = lax.broadcasted_iota(jnp.int32, (tq, LANES), 1)
    first_half = lane < DA_QK_DIM
    rr = lax.broadcasted_iota(jnp.int32, (tq, tq), 0)
    cc = lax.broadcasted_iota(jnp.int32, (tq, tq), 1)
    causal = rr >= cc
    neg = jnp.finfo(F32).min
    zero = jnp.zeros((), BF16)
    for i in reversed(range(SEQ // tq)):
        past = i * tq
        qi = q_ref[past:past + tq, :]
        outs = []
        for qc in (jnp.where(first_half, qi, zero), jnp.where(first_half, zero, qi)):
            s = _dot_nt(qc, k_ref[0:past + tq, :])
            sd = jnp.where(causal, s[:, past:], neg)
            m = jnp.max(sd, axis=-1, keepdims=True)
            if past:
                m = jnp.maximum(m, jnp.max(s[:, :past], axis=-1, keepdims=True))
            p = jnp.exp2(sd - m).astype(BF16)
            if past:
                p = jnp.concatenate([jnp.exp2(s[:, :past] - m).astype(BF16), p], axis=1)
            acc = _dot(p, vaug_ref[0:past + tq, :])
            outs.append(acc[:, 0:DA_V_DIM] / acc[:, DA_V_DIM:DA_V_DIM + 1])
        o = outs[0] - lam * outs[1]
        y = _rms(o, sg_ref[...]) * (1.0 - lam_init)
        o_ref[past:past + tq, :] = y.astype(o_ref.dtype)


def _prompt_attn(q, kb, vb, lq1, lk1, lq2, lk2, sg, *, layer, lam_init):
    n = q.shape[0]
    batch = n // SEQ
    head = pl.BlockSpec((SEQ, DA_V_DIM), lambda b, h: (b, h))
    vec = _layer_spec((1, DA_QK_DIM), layer, 2)
    return pl.pallas_call(
        functools.partial(_prompt_attn_kernel, lam_init=lam_init),
        grid=(batch, DA_HEADS),
        in_specs=[head, head, head, vec, vec, vec, vec, _layer_spec((1, DA_V_DIM), layer, 2)],
        out_specs=head,
        out_shape=jax.ShapeDtypeStruct((n, DA_WIDTH), BF16),
        scratch_shapes=[pltpu.VMEM((SEQ, MXU_DIM), BF16)],
        compiler_params=_params(2),
        name="prompt_attn",
    )(q, kb, vb, lq1, lk1, lq2, lk2, sg)


N_SCORE_ROWS = 2 * DA_HEADS
NEG = float(jnp.finfo(jnp.float32).min)


def _per_row_head(vec_ref):
    rid = lax.broadcasted_iota(jnp.int32, (N_SCORE_ROWS, LANES), 0)
    row_head = lax.shift_right_logical(rid, 1)
    out = jnp.zeros((N_SCORE_ROWS, LANES), F32)
    for h in range(DA_HEADS):
        hs = slice(h * LANES, (h + 1) * LANES)
        out = jnp.where(row_head == h, jnp.broadcast_to(vec_ref[:, hs], (N_SCORE_ROWS, LANES)), out)
    return out


def _decode_query(q_ref):
    rid = lax.broadcasted_iota(jnp.int32, (N_SCORE_ROWS, LANES), 0)
    lid = lax.broadcasted_iota(jnp.int32, (N_SCORE_ROWS, LANES), 1)
    own_half = lax.shift_right_logical(lid, int(math.log2(DA_QK_DIM))) == (rid & 1)
    return jnp.where(own_half, _per_row_head(q_ref), 0.0)


def _decode_query_rows(q_ref):
    rid = lax.broadcasted_iota(jnp.int32, (8, LANES), 0)
    out = jnp.zeros((8, LANES), F32)
    for h in range(DA_HEADS):
        hs = slice(h * LANES, (h + 1) * LANES)
        out = jnp.where((rid & (DA_HEADS - 1)) == h, jnp.broadcast_to(q_ref[:, hs], (8, LANES)), out)
    return out


def _decode_scores(q8, k_pages, state):
    cid = lax.broadcasted_iota(jnp.int32, (N_SCORE_ROWS, PAGE_ROWS), 1)
    rid = lax.broadcasted_iota(jnp.int32, (N_SCORE_ROWS, PAGE_ROWS), 0)
    own_col = (cid & (DA_HEADS - 1)) == lax.shift_right_logical(rid, 1)
    first_half = (rid & 1) == 0
    sub = q8.shape[0]

    def scores(kp):
        prod = (kp.reshape(PAGE_ROWS // sub, sub, LANES) * q8[None]).reshape(PAGE_ROWS, LANES)
        t = prod.T
        s1 = jnp.sum(t[0:DA_QK_DIM], axis=0, keepdims=True)
        s2 = jnp.sum(t[DA_QK_DIM:], axis=0, keepdims=True)
        return jnp.where(own_col, jnp.where(first_half, s1, s2), NEG)

    m_old, l_old, acc_old = state
    s = [scores(kp) for kp in k_pages]
    m_new = m_old
    for sp in s:
        m_new = jnp.maximum(m_new, jnp.max(sp, axis=-1, keepdims=True))
    alpha = jnp.exp2(m_old - m_new)
    l_new = alpha * l_old
    probs = []
    for sp in s:
        pr = jnp.exp2(sp - m_new)
        l_new = l_new + jnp.sum(pr, axis=-1, keepdims=True)
        probs.append(pr.astype(BF16))
    return (m_new, l_new, alpha * acc_old), probs


def _decode_values(state, probs, v_pages):
    m_new, l_new, acc = state
    for pr, vp in zip(probs, v_pages):
        acc = acc + _dot(pr, vp.astype(BF16))
    return m_new, l_new, acc


def _decode_finish(qm, state, kn_ref, vn_ref, lam, sg_ref, o_ref, lam_init):
    m_old, l_old, acc = state
    s_new = jnp.sum(qm * _per_row_head(kn_ref), axis=-1, keepdims=True)
    m_f = jnp.maximum(m_old, s_new)
    a = jnp.exp2(m_old - m_f)
    p_new = jnp.exp2(s_new - m_f)
    l_f = a * l_old + p_new
    full = (a * acc + p_new * _per_row_head(vn_ref)) / l_f
    for h in range(DA_HEADS):
        hs = slice(h * DA_V_DIM, (h + 1) * DA_V_DIM)
        o = full[2 * h:2 * h + 1, :] - lam * full[2 * h + 1:2 * h + 2, :]
        o_ref[:, hs] = (_rms(o, sg_ref[...]) * (1.0 - lam_init)).astype(o_ref.dtype)


def _finish_kernel(*refs, decode, layer, lam_init):
    if decode:
        pt_ref, refs = refs[0], refs[1:]
        (q_ref, kn_ref, vn_ref, lq1_ref, lk1_ref, lq2_ref, lk2_ref, sg_ref, ck_hbm, cv_hbm,
         o_ref, das_ref, kbuf, vbuf, sem) = refs[9:]
    else:
        (o_ref,) = refs[9:]
    x_ref, gm_ref, da_ref, wo_ref, g1_ref, g2_ref, g3_ref, wi_ref, wf_ref = refs[:9]
    vals = {}

    def project_and_norm(between_matmuls):
        mix = _dot(gm_ref[...], wo_ref[0:GM_WIDTH, :]) + _dot(da_ref[...], wo_ref[GM_WIDTH:, :])
        between_matmuls()
        vals["x1"] = x_ref[...] + _rms(mix, g1_ref[...])
        vals["xn"] = _rms(vals["x1"], g2_ref[...]).astype(BF16)

    def ffn_chunk(c0, n, last, between_matmuls):
        gate = _dot(vals["xn"], wi_ref[:, c0:c0 + n])
        up = _dot(vals["xn"], wi_ref[:, D_FF + c0:D_FF + c0 + n])
        between_matmuls()
        act = (jax.nn.silu(gate) * up).astype(BF16)
        part = _dot(act, wf_ref[c0:c0 + n, :])
        vals["f"] = part if c0 == 0 else vals["f"] + part
        if last:
            o_ref[...] = vals["x1"] + _rms(vals["f"], g3_ref[...])

    chunk_pieces = [functools.partial(ffn_chunk, c0, n, k == len(FF_CHUNKS) - 1)
                    for k, (c0, n) in enumerate(FF_CHUNKS)]

    def first_piece(between_matmuls):
        project_and_norm(between_matmuls)
        chunk_pieces[0](lambda: None)

    pieces = [first_piece] + chunk_pieces[1:]

    if not decode:
        for piece in pieces:
            piece(lambda: None)
        return

    i = pl.program_id(0)
    n_groups = len(pieces)
    n_pages = pt_ref.shape[1]
    per_group = n_pages // n_groups
    n_slots = kbuf.shape[0]
    ahead = n_slots - 1
    assert n_pages % n_groups == 0 and kbuf.shape[1] == per_group and ahead <= n_groups

    def page_copy(which, slot, p, page):
        hbm, buf = ((ck_hbm, kbuf), (cv_hbm, vbuf))[which]
        return pltpu.make_async_copy(hbm.at[layer, page], buf.at[slot, p], sem.at[which, slot])

    def start_group(row, g, slot):
        for p in range(per_group):
            page = pt_ref[row, g * per_group + p]
            page_copy(0, slot, p, page).start()
            page_copy(1, slot, p, page).start()

    def wait_group(slot):
        for p in range(per_group):
            page_copy(0, slot, p, 0).wait()
            page_copy(1, slot, p, 0).wait()

    first_slot = lax.rem(i * n_groups, n_slots)
    slot_of = lambda g: lax.rem(first_slot + g, n_slots)

    @pl.when(i == 0)
    def _():
        for g in range(ahead):
            start_group(0, g, g)

    qm = _decode_query(q_ref)
    q8 = _decode_query_rows(q_ref)
    state = (jnp.full((N_SCORE_ROWS, 1), NEG, F32), jnp.zeros((N_SCORE_ROWS, 1), F32),
             jnp.zeros((N_SCORE_ROWS, DA_V_DIM), F32))
    for g, piece in enumerate(pieces):
        slot = slot_of(g)
        if g + ahead < n_groups:
            start_group(i, g + ahead, slot_of(g + ahead))
        else:
            @pl.when(i + 1 < pl.num_programs(0))
            def _():
                start_group(i + 1, g + ahead - n_groups, slot_of(g + ahead))
        wait_group(slot)
        state, probs = _decode_scores(q8, [kbuf[slot, p] for p in range(per_group)], state)

        def add_values(state=state, probs=probs, slot=slot):
            vals["dec"] = _decode_values(state, probs, [vbuf[slot, p] for p in range(per_group)])

        piece(add_values)
        state = vals["dec"]
    lam = _lambda_vec(lq1_ref, lk1_ref, lq2_ref, lk2_ref, lam_init)
    _decode_finish(qm, state, kn_ref, vn_ref, lam, sg_ref, das_ref, lam_init)


def _finish(x, gm, da, w_o, norm_g, w_ffn_in, w_ffn_out, *, layer, tm, decode=None):
    n = x.shape[0]
    assert n % tm == 0
    steps = n // tm
    row = lambda width: pl.BlockSpec((tm, width), lambda i, *_: (i, 0))
    const = lambda shape, first: pl.BlockSpec(
        (None,) + shape, lambda i, *_: (first,) + (0,) * len(shape), pipeline_mode=pl.Buffered(1))
    gain = lambda k: const((1, D_MODEL), 4 * layer + k)
    in_specs = [row(D_MODEL), row(GM_WIDTH), row(DA_WIDTH), const((D_MODEL, D_MODEL), layer),
                gain(1), gain(2), gain(3), const((D_MODEL, 2 * D_FF), layer),
                const((D_FF, D_MODEL), layer)]
    args = [x, gm, da, w_o, norm_g, norm_g, norm_g, w_ffn_in, w_ffn_out]
    x_out = jax.ShapeDtypeStruct((n, D_MODEL), F32)
    if decode is None:
        return pl.pallas_call(
            functools.partial(_finish_kernel, decode=False, layer=layer, lam_init=None),
            grid=(steps,), in_specs=in_specs, out_specs=row(D_MODEL), out_shape=x_out,
            compiler_params=_params(1), name="finish_rows%d" % tm,
        )(*args)

    page_table, q, k_new, v_new, cache_k, cache_v, lams, sg, lam_init = decode
    bs, n_pages = page_table.shape
    assert bs == steps, "one sample row per grid step"
    n_groups = len(FF_CHUNKS)
    per_group = n_pages // n_groups
    rowspec = pl.BlockSpec((None, 1, DA_WIDTH), lambda i, *_: (i, 0, 0))
    in_specs += [rowspec, rowspec, rowspec] + [const((1, DA_QK_DIM), layer)] * 4 + [
        const((1, DA_V_DIM), layer), pl.BlockSpec(memory_space=pl.ANY),
        pl.BlockSpec(memory_space=pl.ANY)]
    args += [q.reshape(bs, 1, DA_WIDTH), k_new.reshape(bs, 1, DA_WIDTH),
             v_new.reshape(bs, 1, DA_WIDTH), *lams, sg, cache_k, cache_v]
    grid_spec = pltpu.PrefetchScalarGridSpec(
        num_scalar_prefetch=1, grid=(steps,), in_specs=in_specs,
        out_specs=[row(D_MODEL), rowspec],
        scratch_shapes=[pltpu.VMEM((RING_SLOTS, per_group, PAGE_ROWS, DA_V_DIM), F32),
                        pltpu.VMEM((RING_SLOTS, per_group, PAGE_ROWS, DA_V_DIM), F32),
                        pltpu.SemaphoreType.DMA((2, RING_SLOTS))],
    )
    x_new, da_s = pl.pallas_call(
        functools.partial(_finish_kernel, decode=True, layer=layer, lam_init=lam_init),
        grid_spec=grid_spec,
        out_shape=[x_out, jax.ShapeDtypeStruct((bs, 1, DA_WIDTH), BF16)],
        compiler_params=pltpu.CompilerParams(dimension_semantics=("arbitrary",),
                                             vmem_limit_bytes=RING_VMEM_LIMIT_BYTES),
        name="finish_decode",
    )(page_table, *args)
    return x_new, da_s.reshape(bs, DA_WIDTH)


def _rope_tables(pos):
    half = ROT_DIM // 2
    inv_freq = ROPE_THETA ** (-jnp.arange(0, ROT_DIM, 2, dtype=F32) / ROT_DIM)
    ang = pos.astype(F32)[:, None] * inv_freq[None, :]
    cos, sin = jnp.cos(ang), jnp.sin(ang)
    npos = pos.shape[0]
    rest = DA_QK_DIM - ROT_DIM
    ones = jnp.ones((npos, rest), F32)
    zeros = jnp.zeros((npos, rest), F32)
    zh = jnp.zeros((npos, half), F32)
    c64 = jnp.concatenate([cos, cos, ones], axis=1)
    a64 = jnp.concatenate([-sin, zh, zeros], axis=1)
    b64 = jnp.concatenate([zh, sin, zeros], axis=1)
    two = lambda t: jnp.concatenate([t, t], axis=1)
    return two(c64), two(a64), two(b64)


def kernel(x_prompt, x_sample, cache_k, cache_v, page_table, w_in, w_gmlp_s, b_gmlp_s, ln_v_g, ln_v_b,
           lambda_q1, lambda_k1, lambda_q2, lambda_k2, subln_g, w_o, norm_g, w_ffn_in, w_ffn_out):
    batch, seq, _ = x_prompt.shape
    bs = x_sample.shape[0]
    n_pool = cache_k.shape[1]
    xp = x_prompt.reshape(batch * seq, D_MODEL)
    xs = x_sample.reshape(bs, D_MODEL)
    ck = cache_k.reshape(DEPTH, n_pool, PAGE_ROWS, DA_V_DIM)
    cv = cache_v.reshape(DEPTH, n_pool, PAGE_ROWS, DA_V_DIM)
    tabs_p = _rope_tables(jnp.arange(SEQ, dtype=jnp.int32))
    tabs_s = _rope_tables(jnp.full((1,), PAST_LEN, dtype=jnp.int32))
    w_in_b = w_in.astype(BF16)
    w_o_b = w_o.astype(BF16)
    w_fi_b = w_ffn_in.astype(BF16)
    w_fo_b = w_ffn_out.astype(BF16)
    gains = norm_g.reshape(DEPTH * 4, 1, D_MODEL)
    lng = ln_v_g.reshape(DEPTH, 1, GM_WIDTH)
    lnb = ln_v_b.reshape(DEPTH, 1, GM_WIDTH)
    bst = jnp.swapaxes(b_gmlp_s, 1, 2)
    lams = [t.reshape(DEPTH, 1, DA_QK_DIM) for t in (lambda_q1, lambda_k1, lambda_q2, lambda_k2)]
    sg = subln_g.reshape(DEPTH, 1, DA_V_DIM)

    kv_p = None
    k_s, v_s, gv_s = [], [], []
    for l in range(DEPTH):
        lam_init = _lambda_init(l)
        gm, q, kb, vb, k_all, v_all = _mixer_in(xp, gains, w_in_b, lng, lnb, w_gmlp_s, bst, *tabs_p,
                                                layer=l, sample=False, kv_prev=kv_p)
        kv_p = (k_all, v_all)
        da = _prompt_attn(q, kb, vb, *lams, sg, layer=l, lam_init=lam_init)
        gm_s, q_s, k, v, vg = _mixer_in(xs, gains, w_in_b, lng, lnb, w_gmlp_s, bst, *tabs_s,
                                        layer=l, sample=True)
        xp, da_s = _finish(xp, gm, da, w_o_b, gains, w_fi_b, w_fo_b, layer=l, tm=ROW_TILE,
                           decode=(page_table, q_s, k, v, ck, cv, lams, sg, lam_init))
        xs = _finish(xs, gm_s, da_s, w_o_b, gains, w_fi_b, w_fo_b, layer=l, tm=bs)
        k_s.append(k.reshape(bs, 1, DA_HEADS, 2 * DA_QK_DIM))
        v_s.append(v.reshape(bs, 1, DA_HEADS, DA_V_DIM))
        gv_s.append(vg.reshape(bs, 1, GM_WIDTH))

    k_prompt = kv_p[0].reshape(DEPTH, batch, seq, DA_HEADS, 2 * DA_QK_DIM)
    v_prompt = kv_p[1].reshape(DEPTH, batch, seq, DA_HEADS, DA_V_DIM)
    return (xp.reshape(batch, seq, D_MODEL), xs.reshape(bs, 1, D_MODEL),
            k_prompt, v_prompt, jnp.stack(k_s), jnp.stack(v_s), jnp.stack(gv_s))
```

```python
import functools
import math

import jax
import jax.numpy as jnp
from jax import lax
from jax.experimental import pallas as pl
from jax.experimental.pallas import tpu as pltpu

D_MODEL = 1024
SEQ = 2048
DEPTH = 4
PAST_LEN = 8192
PAGE_SIZE = 128
GM_WIDTH = 512
GM_HEADS = 4
GM_HEAD_DIM = 128
CHUNK = 128
DA_WIDTH = 512
DA_HEADS = 4
DA_V_DIM = 128
DA_QK_DIM = 64
ROT_DIM = 16
ROPE_THETA = 500000.0
D_FF = 2816
IN_COLS = 2 * GM_WIDTH + 3 * DA_WIDTH
EPS = 1e-6

LANES = 128
MXU_DIM = 256
VMEM_LIMIT_BYTES = 56 * 1024 * 1024

ROW_TILE = 512
Q_TILE = 256
FF_CHUNKS = ((0, 512), (512, 768), (1280, 768), (2048, 768))
PAGE_ROWS = PAGE_SIZE * DA_HEADS
RING_SLOTS = 3
RING_VMEM_LIMIT_BYTES = 61 * 1024 * 1024

Q_SCALE = (DA_QK_DIM ** -0.5) * math.log2(math.e)

F32 = jnp.float32
BF16 = jnp.bfloat16


def _lambda_init(layer_idx):
    return 0.8 - 0.6 * math.exp(-0.3 * layer_idx)


def _rms(x, g):
    return (x * lax.rsqrt(jnp.mean(x * x, axis=-1, keepdims=True) + EPS)) * g


def _dot(a, b):
    return jnp.dot(a, b, preferred_element_type=F32)


def _dot_nt(a, b):
    return lax.dot_general(a, b, (((1,), (1,)), ((), ())), preferred_element_type=F32)


def _lambda_vec(lq1_ref, lk1_ref, lq2_ref, lk2_ref, lam_init):
    a = jnp.sum(lq1_ref[...] * lk1_ref[...], axis=-1, keepdims=True)
    b = jnp.sum(lq2_ref[...] * lk2_ref[...], axis=-1, keepdims=True)
    return jnp.exp(a) - jnp.exp(b) + lam_init


def _layer_spec(shape, layer, grid_rank):
    nd = len(shape)
    if grid_rank == 1:
        index_map = lambda i: (layer,) + (0,) * nd
    else:
        index_map = lambda i, j: (layer,) + (0,) * nd
    return pl.BlockSpec((None,) + tuple(shape), index_map, pipeline_mode=pl.Buffered(1))


def _params(grid_rank):
    return pltpu.CompilerParams(dimension_semantics=("arbitrary",) * grid_rank,
                                vmem_limit_bytes=VMEM_LIMIT_BYTES)


def _mixer_in_kernel(*refs, sample, n_alias, layer):
    (x_ref, g_ref, w_ref, lng_ref, lnb_ref, ws_ref, bst_ref, cos_ref, sa_ref, sb_ref) = refs[:10]
    out_refs = refs[10 + n_alias:]
    if sample:
        gm_ref, q_ref, k_ref, v_ref, vg_ref = out_refs
    else:
        gm_ref, q_ref, kb_ref, vb_ref, k_ref, v_ref = out_refs
        k_out, v_out = (k_ref, v_ref) if n_alias else (k_ref.at[layer], v_ref.at[layer])
    rows = x_ref.shape[0]
    xn = _rms(x_ref[...], g_ref[...]).astype(BF16)

    def proj(c0, n):
        return _dot(xn, w_ref[:, c0:c0 + n])

    o = 2 * GM_WIDTH
    u_raw = proj(0, GM_WIDTH)
    vr_raw = proj(GM_WIDTH, GM_WIDTH)
    qf = proj(o, DA_WIDTH)
    kf = proj(o + DA_WIDTH, DA_WIDTH)
    vf = proj(o + 2 * DA_WIDTH, DA_WIDTH)
    u = jax.nn.gelu(u_raw)
    vr = jax.nn.gelu(vr_raw)
    mu = jnp.mean(vr, axis=-1, keepdims=True)
    var = jnp.mean((vr - mu) ** 2, axis=-1, keepdims=True)
    vg = (vr - mu) * lax.rsqrt(var + EPS) * lng_ref[...] + lnb_ref[...]

    if sample:
        vg_ref[...] = vg
        for h in range(GM_HEADS):
            hs = slice(h * GM_HEAD_DIM, (h + 1) * GM_HEAD_DIM)
            w00 = ws_ref[h, 0:1, 0:1]
            b00 = bst_ref[0:1, h:h + 1]
            mixed = vg[:, hs] * w00 + b00
            gm_ref[:, hs] = (u[:, hs] * mixed).astype(BF16)
    else:
        r = lax.broadcasted_iota(jnp.int32, (CHUNK, CHUNK), 0)
        c = lax.broadcasted_iota(jnp.int32, (CHUNK, CHUNK), 1)
        tril = r >= c
        vgb = vg.astype(BF16)
        for h in range(GM_HEADS):
            hs = slice(h * GM_HEAD_DIM, (h + 1) * GM_HEAD_DIM)
            wsm = jnp.where(tril, ws_ref[h], 0.0).astype(BF16)
            bcol = bst_ref[:, h:h + 1]
            for ci in range(rows // CHUNK):
                rs = slice(ci * CHUNK, (ci + 1) * CHUNK)
                mixed = _dot(wsm, vgb[rs, hs]) + bcol
                gm_ref[rs, hs] = (u[rs, hs] * mixed).astype(BF16)

    cos = cos_ref[...]
    sa = sa_ref[...]
    sb = sb_ref[...]

    def rope(t):
        return (t * cos + pltpu.roll(t, LANES - ROT_DIM // 2, 1) * sa
                + pltpu.roll(t, ROT_DIM // 2, 1) * sb)

    for h in range(DA_HEADS):
        hs = slice(h * DA_V_DIM, (h + 1) * DA_V_DIM)
        q_ref[:, hs] = (rope(qf[:, hs]) * Q_SCALE).astype(q_ref.dtype)
    for h in range(DA_HEADS):
        hs = slice(h * DA_V_DIM, (h + 1) * DA_V_DIM)
        kr = rope(kf[:, hs])
        if sample:
            k_ref[:, hs] = kr
        else:
            k_out[pl.ds(h, rows, stride=DA_HEADS), :] = kr
            v_out[pl.ds(h, rows, stride=DA_HEADS), :] = vf[:, hs]
            kb_ref[:, hs] = kr.astype(BF16)
    if sample:
        v_ref[...] = vf
    else:
        vb_ref[...] = vf.astype(BF16)
        if n_alias == 0:
            for other in range(DEPTH):
                if other != layer:
                    k_ref[other] = jnp.zeros(k_ref.shape[1:], F32)
                    v_ref[other] = jnp.zeros(v_ref.shape[1:], F32)


def _mixer_in(x, norm_g, w_in, lng, lnb, ws, bst, cos, sa, sb, *, layer, sample, kv_prev=None):
    n = x.shape[0]
    tm = n if sample else ROW_TILE
    assert n % tm == 0 and (sample or SEQ % tm == 0)
    steps = n // tm
    tab_steps = 1 if sample else SEQ // tm
    row = lambda width: pl.BlockSpec((tm, width), lambda i: (i, 0))
    tab = pl.BlockSpec((cos.shape[0] // tab_steps, LANES), lambda i: (i % tab_steps, 0))
    in_specs = [
        row(D_MODEL),
        pl.BlockSpec((None, 1, D_MODEL), lambda i: (4 * layer, 0, 0), pipeline_mode=pl.Buffered(1)),
        pl.BlockSpec((D_MODEL, IN_COLS), lambda i: (0, 0), pipeline_mode=pl.Buffered(1)),
        _layer_spec((1, GM_WIDTH), layer, 1), _layer_spec((1, GM_WIDTH), layer, 1),
        _layer_spec((GM_HEADS, CHUNK, CHUNK), layer, 1), _layer_spec((CHUNK, GM_HEADS), layer, 1),
        tab, tab, tab,
    ]
    args = [x, norm_g, w_in, lng, lnb, ws, bst, cos, sa, sb]
    sds = jax.ShapeDtypeStruct
    aliases = {}
    if sample:
        out_shape = [sds((n, GM_WIDTH), BF16), sds((n, DA_WIDTH), F32), sds((n, DA_WIDTH), F32),
                     sds((n, DA_WIDTH), F32), sds((n, GM_WIDTH), F32)]
        out_specs = [row(s.shape[1]) for s in out_shape]
    else:
        small = [sds((n, GM_WIDTH), BF16), sds((n, DA_WIDTH), BF16), sds((n, DA_WIDTH), BF16),
                 sds((n, DA_WIDTH), BF16)]
        big = sds((DEPTH, n * DA_HEADS, DA_V_DIM), F32)
        out_shape = small + [big, big]
        if kv_prev is None:
            big_spec = pl.BlockSpec((DEPTH, tm * DA_HEADS, DA_V_DIM), lambda i: (0, i, 0))
        else:
            big_spec = pl.BlockSpec((None, tm * DA_HEADS, DA_V_DIM), lambda i: (layer, i, 0))
            for t in kv_prev:
                aliases[len(args)] = len(small) + len(aliases)
                args.append(t)
                in_specs.append(pl.BlockSpec(memory_space=pl.ANY))
        out_specs = [row(s.shape[1]) for s in small] + [big_spec, big_spec]
    return pl.pallas_call(
        functools.partial(_mixer_in_kernel, sample=sample, n_alias=len(aliases), layer=layer),
        grid=(steps,), in_specs=in_specs, out_specs=out_specs, out_shape=out_shape,
        input_output_aliases=aliases, compiler_params=_params(1),
        name="mixer_in_sample" if sample else "mixer_in_prompt",
    )(*args)


def _prompt_attn_kernel(q_ref, k_ref, v_ref, lq1_ref, lk1_ref, lq2_ref, lk2_ref, sg_ref, *rest,
                        lam_init, n_convert):
    f32_refs, o_ref = rest[:n_convert], rest[n_convert]
    bf16_refs, vaug_ref = rest[n_convert + 1:2 * n_convert + 1], rest[2 * n_convert + 1]
    for src, dst in zip(f32_refs, bf16_refs):
        dst[...] = src[...].astype(BF16)
    lam = _lambda_vec(lq1_ref, lk1_ref, lq2_ref, lk2_ref, lam_init)
    tq = Q_TILE
    vaug_ref[:, 0:DA_V_DIM] = v_ref[...]
    vaug_ref[:, DA_V_DIM:] = jnp.ones((SEQ, MXU_DIM - DA_V_DIM), BF16)
    lane = lax.broadcasted_iota(jnp.int32, (tq, LANES), 1)
    first_half = lane < DA_QK_DIM
    rr = lax.broadcasted_iota(jnp.int32, (tq, tq), 0)
    cc = lax.broadcasted_iota(jnp.int32, (tq, tq), 1)
    causal = rr >= cc
    neg = jnp.finfo(F32).min
    zero = jnp.zeros((), BF16)
    for i in reversed(range(SEQ // tq)):
        past = i * tq
        qi = q_ref[past:past + tq, :]
        outs = []
        for qc in (jnp.where(first_half, qi, zero), jnp.where(first_half, zero, qi)):
            s = _dot_nt(qc, k_ref[0:past + tq, :])
            sd = jnp.where(causal, s[:, past:], neg)
            m = jnp.max(sd, axis=-1, keepdims=True)
            if past:
                m = jnp.maximum(m, jnp.max(s[:, :past], axis=-1, keepdims=True))
            p = jnp.exp2(sd - m).astype(BF16)
            if past:
                p = jnp.concatenate([jnp.exp2(s[:, :past] - m).astype(BF16), p], axis=1)
            acc = _dot(p, vaug_ref[0:past + tq, :])
            outs.append(acc[:, 0:DA_V_DIM] / acc[:, DA_V_DIM:DA_V_DIM + 1])
        o = outs[0] - lam * outs[1]
        y = _rms(o, sg_ref[...]) * (1.0 - lam_init)
        o_ref[past:past + tq, :] = y.astype(o_ref.dtype)


def _prompt_attn(q, kb, vb, lq1, lk1, lq2, lk2, sg, *, layer, lam_init, convert=()):
    n = q.shape[0]
    batch = n // SEQ
    steps = batch * DA_HEADS
    head = pl.BlockSpec((SEQ, DA_V_DIM), lambda b, h: (b, h))
    vec = _layer_spec((1, DA_QK_DIM), layer, 2)
    in_specs = [head, head, head, vec, vec, vec, vec, _layer_spec((1, DA_V_DIM), layer, 2)]
    out_specs = [head]
    out_shape = [jax.ShapeDtypeStruct((n, DA_WIDTH), BF16)]
    for w in convert:
        _, rows, cols = w.shape
        n_blocks = steps if rows % (steps * 16) == 0 else steps // 2
        assert rows % (n_blocks * 16) == 0
        rb = rows // n_blocks
        blk = lambda b, h, per=steps // n_blocks: (b * DA_HEADS + h) // per
        in_specs.append(pl.BlockSpec((None, rb, cols),
                                     lambda b, h, blk=blk: (layer + 1, blk(b, h), 0)))
        out_specs.append(pl.BlockSpec((rb, cols), lambda b, h, blk=blk: (blk(b, h), 0)))
        out_shape.append(jax.ShapeDtypeStruct((rows, cols), BF16))
    outs = pl.pallas_call(
        functools.partial(_prompt_attn_kernel, lam_init=lam_init, n_convert=len(convert)),
        grid=(batch, DA_HEADS),
        in_specs=in_specs, out_specs=out_specs, out_shape=out_shape,
        scratch_shapes=[pltpu.VMEM((SEQ, MXU_DIM), BF16)],
        compiler_params=_params(2),
        name="prompt_attn",
    )(q, kb, vb, lq1, lk1, lq2, lk2, sg, *convert)
    return outs[0], outs[1:]


N_SCORE_ROWS = 2 * DA_HEADS
NEG = float(jnp.finfo(jnp.float32).min)


def _per_row_head(vec_ref):
    rid = lax.broadcasted_iota(jnp.int32, (N_SCORE_ROWS, LANES), 0)
    row_head = lax.shift_right_logical(rid, 1)
    out = jnp.zeros((N_SCORE_ROWS, LANES), F32)
    for h in range(DA_HEADS):
        hs = slice(h * LANES, (h + 1) * LANES)
        out = jnp.where(row_head == h, jnp.broadcast_to(vec_ref[:, hs], (N_SCORE_ROWS, LANES)), out)
    return out


def _decode_query(q_ref):
    rid = lax.broadcasted_iota(jnp.int32, (N_SCORE_ROWS, LANES), 0)
    lid = lax.broadcasted_iota(jnp.int32, (N_SCORE_ROWS, LANES), 1)
    own_half = lax.shift_right_logical(lid, int(math.log2(DA_QK_DIM))) == (rid & 1)
    return jnp.where(own_half, _per_row_head(q_ref), 0.0)


def _decode_query_rows(q_ref):
    rid = lax.broadcasted_iota(jnp.int32, (8, LANES), 0)
    out = jnp.zeros((8, LANES), F32)
    for h in range(DA_HEADS):
        hs = slice(h * LANES, (h + 1) * LANES)
        out = jnp.where((rid & (DA_HEADS - 1)) == h, jnp.broadcast_to(q_ref[:, hs], (8, LANES)), out)
    return out


def _decode_scores(q8, k_pages, state):
    cid = lax.broadcasted_iota(jnp.int32, (N_SCORE_ROWS, PAGE_ROWS), 1)
    rid = lax.broadcasted_iota(jnp.int32, (N_SCORE_ROWS, PAGE_ROWS), 0)
    own_col = (cid & (DA_HEADS - 1)) == lax.shift_right_logical(rid, 1)
    first_half = (rid & 1) == 0
    sub = q8.shape[0]

    def scores(kp):
        prod = (kp.reshape(PAGE_ROWS // sub, sub, LANES) * q8[None]).reshape(PAGE_ROWS, LANES)
        t = prod.T
        s1 = jnp.sum(t[0:DA_QK_DIM], axis=0, keepdims=True)
        s2 = jnp.sum(t[DA_QK_DIM:], axis=0, keepdims=True)
        return jnp.where(own_col, jnp.where(first_half, s1, s2), NEG)

    m_old, l_old, acc_old = state
    s = [scores(kp) for kp in k_pages]
    m_new = m_old
    for sp in s:
        m_new = jnp.maximum(m_new, jnp.max(sp, axis=-1, keepdims=True))
    alpha = jnp.exp2(m_old - m_new)
    l_new = alpha * l_old
    probs = []
    for sp in s:
        pr = jnp.exp2(sp - m_new)
        l_new = l_new + jnp.sum(pr, axis=-1, keepdims=True)
        probs.append(pr.astype(BF16))
    return (m_new, l_new, alpha * acc_old), probs


def _decode_values(state, probs, v_pages):
    m_new, l_new, acc = state
    for pr, vp in zip(probs, v_pages):
        acc = acc + _dot(pr, vp.astype(BF16))
    return m_new, l_new, acc


def _decode_finish(qm, state, kn_ref, vn_ref, lam, sg_ref, o_ref, lam_init):
    m_old, l_old, acc = state
    s_new = jnp.sum(qm * _per_row_head(kn_ref), axis=-1, keepdims=True)
    m_f = jnp.maximum(m_old, s_new)
    a = jnp.exp2(m_old - m_f)
    p_new = jnp.exp2(s_new - m_f)
    l_f = a * l_old + p_new
    full = (a * acc + p_new * _per_row_head(vn_ref)) / l_f
    for h in range(DA_HEADS):
        hs = slice(h * DA_V_DIM, (h + 1) * DA_V_DIM)
        o = full[2 * h:2 * h + 1, :] - lam * full[2 * h + 1:2 * h + 2, :]
        o_ref[:, hs] = (_rms(o, sg_ref[...]) * (1.0 - lam_init)).astype(o_ref.dtype)


def _finish_kernel(*refs, decode, layer, lam_init):
    if decode:
        pt_ref, refs = refs[0], refs[1:]
        (q_ref, kn_ref, vn_ref, lq1_ref, lk1_ref, lq2_ref, lk2_ref, sg_ref, ck_hbm, cv_hbm,
         o_ref, das_ref, kbuf, vbuf, sem) = refs[9:]
    else:
        (o_ref,) = refs[9:]
    x_ref, gm_ref, da_ref, wo_ref, g1_ref, g2_ref, g3_ref, wi_ref, wf_ref = refs[:9]
    vals = {}

    def project_and_norm(between_matmuls):
        mix = _dot(gm_ref[...], wo_ref[0:GM_WIDTH, :]) + _dot(da_ref[...], wo_ref[GM_WIDTH:, :])
        between_matmuls()
        vals["x1"] = x_ref[...] + _rms(mix, g1_ref[...])
        vals["xn"] = _rms(vals["x1"], g2_ref[...]).astype(BF16)

    def ffn_chunk(c0, n, last, between_matmuls):
        gate = _dot(vals["xn"], wi_ref[:, c0:c0 + n])
        up = _dot(vals["xn"], wi_ref[:, D_FF + c0:D_FF + c0 + n])
        between_matmuls()
        act = (jax.nn.silu(gate) * up).astype(BF16)
        part = _dot(act, wf_ref[c0:c0 + n, :])
        vals["f"] = part if c0 == 0 else vals["f"] + part
        if last:
            o_ref[...] = vals["x1"] + _rms(vals["f"], g3_ref[...])

    chunk_pieces = [functools.partial(ffn_chunk, c0, n, k == len(FF_CHUNKS) - 1)
                    for k, (c0, n) in enumerate(FF_CHUNKS)]

    def first_piece(between_matmuls):
        project_and_norm(between_matmuls)
        chunk_pieces[0](lambda: None)

    pieces = [first_piece] + chunk_pieces[1:]

    if not decode:
        for piece in pieces:
            piece(lambda: None)
        return

    i = pl.program_id(0)
    n_groups = len(pieces)
    n_pages = pt_ref.shape[1]
    per_group = n_pages // n_groups
    n_slots = kbuf.shape[0]
    ahead = n_slots - 1
    assert n_pages % n_groups == 0 and kbuf.shape[1] == per_group and ahead <= n_groups

    def page_copy(which, slot, p, page):
        hbm, buf = ((ck_hbm, kbuf), (cv_hbm, vbuf))[which]
        return pltpu.make_async_copy(hbm.at[layer, page], buf.at[slot, p], sem.at[which, slot])

    def start_group(row, g, slot):
        for p in range(per_group):
            page = pt_ref[row, g * per_group + p]
            page_copy(0, slot, p, page).start()
            page_copy(1, slot, p, page).start()

    def wait_group(slot):
        for p in range(per_group):
            page_copy(0, slot, p, 0).wait()
            page_copy(1, slot, p, 0).wait()

    first_slot = lax.rem(i * n_groups, n_slots)
    slot_of = lambda g: lax.rem(first_slot + g, n_slots)

    @pl.when(i == 0)
    def _():
        for g in range(ahead):
            start_group(0, g, g)

    qm = _decode_query(q_ref)
    q8 = _decode_query_rows(q_ref)
    state = (jnp.full((N_SCORE_ROWS, 1), NEG, F32), jnp.zeros((N_SCORE_ROWS, 1), F32),
             jnp.zeros((N_SCORE_ROWS, DA_V_DIM), F32))
    for g, piece in enumerate(pieces):
        slot = slot_of(g)
        if g + ahead < n_groups:
            start_group(i, g + ahead, slot_of(g + ahead))
        else:
            @pl.when(i + 1 < pl.num_programs(0))
            def _():
                start_group(i + 1, g + ahead - n_groups, slot_of(g + ahead))
        wait_group(slot)
        state, probs = _decode_scores(q8, [kbuf[slot, p] for p in range(per_group)], state)

        def add_values(state=state, probs=probs, slot=slot):
            vals["dec"] = _decode_values(state, probs, [vbuf[slot, p] for p in range(per_group)])

        piece(add_values)
        state = vals["dec"]
    lam = _lambda_vec(lq1_ref, lk1_ref, lq2_ref, lk2_ref, lam_init)
    _decode_finish(qm, state, kn_ref, vn_ref, lam, sg_ref, das_ref, lam_init)


def _finish(x, gm, da, w_o, norm_g, w_ffn_in, w_ffn_out, *, layer, tm, decode=None):
    n = x.shape[0]
    assert n % tm == 0
    steps = n // tm
    row = lambda width: pl.BlockSpec((tm, width), lambda i, *_: (i, 0))
    const = lambda shape, first: pl.BlockSpec(
        (None,) + shape, lambda i, *_: (first,) + (0,) * len(shape), pipeline_mode=pl.Buffered(1))
    gain = lambda k: const((1, D_MODEL), 4 * layer + k)
    whole = lambda shape: pl.BlockSpec(shape, lambda i, *_: (0,) * len(shape),
                                       pipeline_mode=pl.Buffered(1))
    in_specs = [row(D_MODEL), row(GM_WIDTH), row(DA_WIDTH), whole((D_MODEL, D_MODEL)),
                gain(1), gain(2), gain(3), whole((D_MODEL, 2 * D_FF)), whole((D_FF, D_MODEL))]
    args = [x, gm, da, w_o, norm_g, norm_g, norm_g, w_ffn_in, w_ffn_out]
    x_out = jax.ShapeDtypeStruct((n, D_MODEL), F32)
    if decode is None:
        return pl.pallas_call(
            functools.partial(_finish_kernel, decode=False, layer=layer, lam_init=None),
            grid=(steps,), in_specs=in_specs, out_specs=row(D_MODEL), out_shape=x_out,
            compiler_params=_params(1), name="finish_rows%d" % tm,
        )(*args)

    page_table, q, k_new, v_new, cache_k, cache_v, lams, sg, lam_init = decode
    bs, n_pages = page_table.shape
    assert bs == steps, "one sample row per grid step"
    n_groups = len(FF_CHUNKS)
    per_group = n_pages // n_groups
    rowspec = pl.BlockSpec((None, 1, DA_WIDTH), lambda i, *_: (i, 0, 0))
    in_specs += [rowspec, rowspec, rowspec] + [const((1, DA_QK_DIM), layer)] * 4 + [
        const((1, DA_V_DIM), layer), pl.BlockSpec(memory_space=pl.ANY),
        pl.BlockSpec(memory_space=pl.ANY)]
    args += [q.reshape(bs, 1, DA_WIDTH), k_new.reshape(bs, 1, DA_WIDTH),
             v_new.reshape(bs, 1, DA_WIDTH), *lams, sg, cache_k, cache_v]
    grid_spec = pltpu.PrefetchScalarGridSpec(
        num_scalar_prefetch=1, grid=(steps,), in_specs=in_specs,
        out_specs=[row(D_MODEL), rowspec],
        scratch_shapes=[pltpu.VMEM((RING_SLOTS, per_group, PAGE_ROWS, DA_V_DIM), F32),
                        pltpu.VMEM((RING_SLOTS, per_group, PAGE_ROWS, DA_V_DIM), F32),
                        pltpu.SemaphoreType.DMA((2, RING_SLOTS))],
    )
    x_new, da_s = pl.pallas_call(
        functools.partial(_finish_kernel, decode=True, layer=layer, lam_init=lam_init),
        grid_spec=grid_spec,
        out_shape=[x_out, jax.ShapeDtypeStruct((bs, 1, DA_WIDTH), BF16)],
        compiler_params=pltpu.CompilerParams(dimension_semantics=("arbitrary",),
                                             vmem_limit_bytes=RING_VMEM_LIMIT_BYTES),
        name="finish_decode",
    )(page_table, *args)
    return x_new, da_s.reshape(bs, DA_WIDTH)


def _rope_tables(pos):
    half = ROT_DIM // 2
    inv_freq = ROPE_THETA ** (-jnp.arange(0, ROT_DIM, 2, dtype=F32) / ROT_DIM)
    ang = pos.astype(F32)[:, None] * inv_freq[None, :]
    cos, sin = jnp.cos(ang), jnp.sin(ang)
    npos = pos.shape[0]
    rest = DA_QK_DIM - ROT_DIM
    ones = jnp.ones((npos, rest), F32)
    zeros = jnp.zeros((npos, rest), F32)
    zh = jnp.zeros((npos, half), F32)
    c64 = jnp.concatenate([cos, cos, ones], axis=1)
    a64 = jnp.concatenate([-sin, zh, zeros], axis=1)
    b64 = jnp.concatenate([zh, sin, zeros], axis=1)
    two = lambda t: jnp.concatenate([t, t], axis=1)
    return two(c64), two(a64), two(b64)


def kernel(x_prompt, x_sample, cache_k, cache_v, page_table, w_in, w_gmlp_s, b_gmlp_s, ln_v_g, ln_v_b,
           lambda_q1, lambda_k1, lambda_q2, lambda_k2, subln_g, w_o, norm_g, w_ffn_in, w_ffn_out):
    batch, seq, _ = x_prompt.shape
    bs = x_sample.shape[0]
    n_pool = cache_k.shape[1]
    xp = x_prompt.reshape(batch * seq, D_MODEL)
    xs = x_sample.reshape(bs, D_MODEL)
    ck = cache_k.reshape(DEPTH, n_pool, PAGE_ROWS, DA_V_DIM)
    cv = cache_v.reshape(DEPTH, n_pool, PAGE_ROWS, DA_V_DIM)
    tabs_p = _rope_tables(jnp.arange(SEQ, dtype=jnp.int32))
    tabs_s = _rope_tables(jnp.full((1,), PAST_LEN, dtype=jnp.int32))
    weights = (w_in, w_o, w_ffn_in, w_ffn_out)
    w_in_b, w_o_b, w_fi_b, w_fo_b = [w[0].astype(BF16) for w in weights]
    gains = norm_g.reshape(DEPTH * 4, 1, D_MODEL)
    lng = ln_v_g.reshape(DEPTH, 1, GM_WIDTH)
    lnb = ln_v_b.reshape(DEPTH, 1, GM_WIDTH)
    bst = jnp.swapaxes(b_gmlp_s, 1, 2)
    lams = [t.reshape(DEPTH, 1, DA_QK_DIM) for t in (lambda_q1, lambda_k1, lambda_q2, lambda_k2)]
    sg = subln_g.reshape(DEPTH, 1, DA_V_DIM)

    kv_p = None
    k_s, v_s, gv_s = [], [], []
    for l in range(DEPTH):
        lam_init = _lambda_init(l)
        gm, q, kb, vb, k_all, v_all = _mixer_in(xp, gains, w_in_b, lng, lnb, w_gmlp_s, bst, *tabs_p,
                                                layer=l, sample=False, kv_prev=kv_p)
        kv_p = (k_all, v_all)
        da, next_w = _prompt_attn(q, kb, vb, *lams, sg, layer=l, lam_init=lam_init,
                                  convert=weights if l + 1 < DEPTH else ())
        gm_s, q_s, k, v, vg = _mixer_in(xs, gains, w_in_b, lng, lnb, w_gmlp_s, bst, *tabs_s,
                                        layer=l, sample=True)
        xp, da_s = _finish(xp, gm, da, w_o_b, gains, w_fi_b, w_fo_b, layer=l, tm=ROW_TILE,
                           decode=(page_table, q_s, k, v, ck, cv, lams, sg, lam_init))
        xs = _finish(xs, gm_s, da_s, w_o_b, gains, w_fi_b, w_fo_b, layer=l, tm=bs)
        if next_w:
            w_in_b, w_o_b, w_fi_b, w_fo_b = next_w
        k_s.append(k.reshape(bs, 1, DA_HEADS, 2 * DA_QK_DIM))
        v_s.append(v.reshape(bs, 1, DA_HEADS, DA_V_DIM))
        gv_s.append(vg.reshape(bs, 1, GM_WIDTH))

    k_prompt = kv_p[0].reshape(DEPTH, batch, seq, DA_HEADS, 2 * DA_QK_DIM)
    v_prompt = kv_p[1].reshape(DEPTH, batch, seq, DA_HEADS, DA_V_DIM)
    return (xp.reshape(batch, seq, D_MODEL), xs.reshape(bs, 1, D_MODEL),
            k_prompt, v_prompt, jnp.stack(k_s), jnp.stack(v_s), jnp.stack(gv_s))
```

```python
import functools
import math

import jax
import jax.numpy as jnp
from jax import lax
from jax.experimental import pallas as pl
from jax.experimental.pallas import tpu as pltpu

D_MODEL = 1024
SEQ = 2048
DEPTH = 4
PAST_LEN = 8192
PAGE_SIZE = 128
GM_WIDTH = 512
GM_HEADS = 4
GM_HEAD_DIM = 128
CHUNK = 128
DA_WIDTH = 512
DA_HEADS = 4
DA_V_DIM = 128
DA_QK_DIM = 64
ROT_DIM = 16
ROPE_THETA = 500000.0
D_FF = 2816
IN_COLS = 2 * GM_WIDTH + 3 * DA_WIDTH
EPS = 1e-6

LANES = 128
MXU_DIM = 256
VMEM_LIMIT_BYTES = 56 * 1024 * 1024

ROW_TILE = 512
Q_TILE = 256
FF_CHUNKS = ((0, 512), (512, 768), (1280, 768), (2048, 768))
PAGE_ROWS = PAGE_SIZE * DA_HEADS
RING_SLOTS = 3
RING_VMEM_LIMIT_BYTES = 61 * 1024 * 1024

Q_SCALE = (DA_QK_DIM ** -0.5) * math.log2(math.e)

F32 = jnp.float32
BF16 = jnp.bfloat16


def _lambda_init(layer_idx):
    return 0.8 - 0.6 * math.exp(-0.3 * layer_idx)


def _rms(x, g):
    return (x * lax.rsqrt(jnp.mean(x * x, axis=-1, keepdims=True) + EPS)) * g


def _dot(a, b):
    return jnp.dot(a, b, preferred_element_type=F32)


def _dot_nt(a, b):
    return lax.dot_general(a, b, (((1,), (1,)), ((), ())), preferred_element_type=F32)


def _lambda_vec(lq1_ref, lk1_ref, lq2_ref, lk2_ref, lam_init):
    a = jnp.sum(lq1_ref[...] * lk1_ref[...], axis=-1, keepdims=True)
    b = jnp.sum(lq2_ref[...] * lk2_ref[...], axis=-1, keepdims=True)
    return jnp.exp(a) - jnp.exp(b) + lam_init


def _layer_spec(shape, layer, grid_rank):
    nd = len(shape)
    if grid_rank == 1:
        index_map = lambda i: (layer,) + (0,) * nd
    else:
        index_map = lambda i, j: (layer,) + (0,) * nd
    return pl.BlockSpec((None,) + tuple(shape), index_map, pipeline_mode=pl.Buffered(1))


def _params(grid_rank):
    return pltpu.CompilerParams(dimension_semantics=("arbitrary",) * grid_rank,
                                vmem_limit_bytes=VMEM_LIMIT_BYTES)


def _mixer_in_kernel(*refs, sample, n_alias, layer):
    (x_ref, g_ref, w_ref, lng_ref, lnb_ref, ws_ref, bst_ref, cos_ref, sa_ref, sb_ref) = refs[:10]
    out_refs = refs[10 + n_alias:]
    if sample:
        gm_ref, q_ref, k_ref, v_ref, vg_ref = out_refs
    else:
        gm_ref, q_ref, kb_ref, vb_ref, k_ref, v_ref = out_refs
        k_out, v_out = (k_ref, v_ref) if n_alias else (k_ref.at[layer], v_ref.at[layer])
    rows = x_ref.shape[0]
    xn = _rms(x_ref[...], g_ref[...]).astype(BF16)

    def proj(c0, n):
        return _dot(xn, w_ref[:, c0:c0 + n])

    o = 2 * GM_WIDTH
    u_raw = proj(0, GM_WIDTH)
    vr_raw = proj(GM_WIDTH, GM_WIDTH)
    qf = proj(o, DA_WIDTH)
    kf = proj(o + DA_WIDTH, DA_WIDTH)
    vf = proj(o + 2 * DA_WIDTH, DA_WIDTH)
    u = jax.nn.gelu(u_raw)
    vr = jax.nn.gelu(vr_raw)
    mu = jnp.mean(vr, axis=-1, keepdims=True)
    var = jnp.mean((vr - mu) ** 2, axis=-1, keepdims=True)
    vg = (vr - mu) * lax.rsqrt(var + EPS) * lng_ref[...] + lnb_ref[...]

    if sample:
        vg_ref[...] = vg
        for h in range(GM_HEADS):
            hs = slice(h * GM_HEAD_DIM, (h + 1) * GM_HEAD_DIM)
            w00 = ws_ref[h, 0:1, 0:1]
            b00 = bst_ref[0:1, h:h + 1]
            mixed = vg[:, hs] * w00 + b00
            gm_ref[:, hs] = (u[:, hs] * mixed).astype(BF16)
    else:
        r = lax.broadcasted_iota(jnp.int32, (CHUNK, CHUNK), 0)
        c = lax.broadcasted_iota(jnp.int32, (CHUNK, CHUNK), 1)
        tril = r >= c
        vgb = vg.astype(BF16)
        for h in range(GM_HEADS):
            hs = slice(h * GM_HEAD_DIM, (h + 1) * GM_HEAD_DIM)
            wsm = jnp.where(tril, ws_ref[h], 0.0).astype(BF16)
            bcol = bst_ref[:, h:h + 1]
            for ci in range(rows // CHUNK):
                rs = slice(ci * CHUNK, (ci + 1) * CHUNK)
                mixed = _dot(wsm, vgb[rs, hs]) + bcol
                gm_ref[rs, hs] = (u[rs, hs] * mixed).astype(BF16)

    cos = cos_ref[...]
    sa = sa_ref[...]
    sb = sb_ref[...]

    def rope(t):
        return (t * cos + pltpu.roll(t, LANES - ROT_DIM // 2, 1) * sa
                + pltpu.roll(t, ROT_DIM // 2, 1) * sb)

    for h in range(DA_HEADS):
        hs = slice(h * DA_V_DIM, (h + 1) * DA_V_DIM)
        qh = (rope(qf[:, hs]) * Q_SCALE).astype(q_ref.dtype)
        if sample:
            q_ref[:, hs] = qh
        else:
            q_ref[h] = qh
    for h in range(DA_HEADS):
        hs = slice(h * DA_V_DIM, (h + 1) * DA_V_DIM)
        kr = rope(kf[:, hs])
        if sample:
            k_ref[:, hs] = kr
        else:
            k_out[pl.ds(h, rows, stride=DA_HEADS), :] = kr
            v_out[pl.ds(h, rows, stride=DA_HEADS), :] = vf[:, hs]
            kb_ref[h] = kr.astype(BF16)
            vb_ref[h] = vf[:, hs].astype(BF16)
    if sample:
        v_ref[...] = vf
    else:
        if n_alias == 0:
            for other in range(DEPTH):
                if other != layer:
                    k_ref[other] = jnp.zeros(k_ref.shape[1:], F32)
                    v_ref[other] = jnp.zeros(v_ref.shape[1:], F32)


def _mixer_in(x, norm_g, w_in, lng, lnb, ws, bst, cos, sa, sb, *, layer, sample, kv_prev=None):
    n = x.shape[0]
    tm = n if sample else ROW_TILE
    assert n % tm == 0 and (sample or SEQ % tm == 0)
    steps = n // tm
    tab_steps = 1 if sample else SEQ // tm
    row = lambda width: pl.BlockSpec((tm, width), lambda i: (i, 0))
    tab = pl.BlockSpec((cos.shape[0] // tab_steps, LANES), lambda i: (i % tab_steps, 0))
    in_specs = [
        row(D_MODEL),
        pl.BlockSpec((None, 1, D_MODEL), lambda i: (4 * layer, 0, 0), pipeline_mode=pl.Buffered(1)),
        pl.BlockSpec((D_MODEL, IN_COLS), lambda i: (0, 0), pipeline_mode=pl.Buffered(1)),
        _layer_spec((1, GM_WIDTH), layer, 1), _layer_spec((1, GM_WIDTH), layer, 1),
        _layer_spec((GM_HEADS, CHUNK, CHUNK), layer, 1), _layer_spec((CHUNK, GM_HEADS), layer, 1),
        tab, tab, tab,
    ]
    args = [x, norm_g, w_in, lng, lnb, ws, bst, cos, sa, sb]
    sds = jax.ShapeDtypeStruct
    aliases = {}
    if sample:
        out_shape = [sds((n, GM_WIDTH), BF16), sds((n, DA_WIDTH), F32), sds((n, DA_WIDTH), F32),
                     sds((n, DA_WIDTH), F32), sds((n, GM_WIDTH), F32)]
        out_specs = [row(s.shape[1]) for s in out_shape]
    else:
        per_head = sds((DA_HEADS, n, DA_V_DIM), BF16)
        small = [sds((n, GM_WIDTH), BF16), per_head, per_head, per_head]
        big = sds((DEPTH, n * DA_HEADS, DA_V_DIM), F32)
        out_shape = small + [big, big]
        if kv_prev is None:
            big_spec = pl.BlockSpec((DEPTH, tm * DA_HEADS, DA_V_DIM), lambda i: (0, i, 0))
        else:
            big_spec = pl.BlockSpec((None, tm * DA_HEADS, DA_V_DIM), lambda i: (layer, i, 0))
            for t in kv_prev:
                aliases[len(args)] = len(small) + len(aliases)
                args.append(t)
                in_specs.append(pl.BlockSpec(memory_space=pl.ANY))
        head_spec = pl.BlockSpec((DA_HEADS, tm, DA_V_DIM), lambda i: (0, i, 0))
        out_specs = [row(GM_WIDTH), head_spec, head_spec, head_spec, big_spec, big_spec]
    return pl.pallas_call(
        functools.partial(_mixer_in_kernel, sample=sample, n_alias=len(aliases), layer=layer),
        grid=(steps,), in_specs=in_specs, out_specs=out_specs, out_shape=out_shape,
        input_output_aliases=aliases, compiler_params=_params(1),
        name="mixer_in_sample" if sample else "mixer_in_prompt",
    )(*args)


def _prompt_attn_kernel(q_ref, k_ref, v_ref, lq1_ref, lk1_ref, lq2_ref, lk2_ref, sg_ref, *rest,
                        lam_init, n_convert):
    f32_refs, o_ref = rest[:n_convert], rest[n_convert]
    bf16_refs, vaug_ref = rest[n_convert + 1:2 * n_convert + 1], rest[2 * n_convert + 1]
    for src, dst in zip(f32_refs, bf16_refs):
        dst[...] = src[...].astype(BF16)
    lam = _lambda_vec(lq1_ref, lk1_ref, lq2_ref, lk2_ref, lam_init)
    tq = Q_TILE
    vaug_ref[:, 0:DA_V_DIM] = v_ref[...]
    vaug_ref[:, DA_V_DIM:] = jnp.ones((SEQ, MXU_DIM - DA_V_DIM), BF16)
    lane = lax.broadcasted_iota(jnp.int32, (tq, LANES), 1)
    first_half = lane < DA_QK_DIM
    rr = lax.broadcasted_iota(jnp.int32, (tq, tq), 0)
    cc = lax.broadcasted_iota(jnp.int32, (tq, tq), 1)
    causal = rr >= cc
    neg = jnp.finfo(F32).min
    zero = jnp.zeros((), BF16)
    for i in reversed(range(SEQ // tq)):
        past = i * tq
        qi = q_ref[past:past + tq, :]
        outs = []
        for qc in (jnp.where(first_half, qi, zero), jnp.where(first_half, zero, qi)):
            s = _dot_nt(qc, k_ref[0:past + tq, :])
            sd = jnp.where(causal, s[:, past:], neg)
            m = jnp.max(sd, axis=-1, keepdims=True)
            if past:
                m = jnp.maximum(m, jnp.max(s[:, :past], axis=-1, keepdims=True))
            p = jnp.exp2(sd - m).astype(BF16)
            if past:
                p = jnp.concatenate([jnp.exp2(s[:, :past] - m).astype(BF16), p], axis=1)
            acc = _dot(p, vaug_ref[0:past + tq, :])
            outs.append(acc[:, 0:DA_V_DIM] / acc[:, DA_V_DIM:DA_V_DIM + 1])
        o = outs[0] - lam * outs[1]
        y = _rms(o, sg_ref[...]) * (1.0 - lam_init)
        o_ref[past:past + tq, :] = y.astype(o_ref.dtype)


def _prompt_attn(q, kb, vb, lq1, lk1, lq2, lk2, sg, *, layer, lam_init, convert=()):
    n = q.shape[1]
    batch = n // SEQ
    steps = batch * DA_HEADS
    head = pl.BlockSpec((SEQ, DA_V_DIM), lambda b, h: (b, h))
    slab = pl.BlockSpec((None, SEQ, DA_V_DIM), lambda b, h: (h, b, 0))
    vec = _layer_spec((1, DA_QK_DIM), layer, 2)
    in_specs = [slab, slab, slab, vec, vec, vec, vec, _layer_spec((1, DA_V_DIM), layer, 2)]
    out_specs = [head]
    out_shape = [jax.ShapeDtypeStruct((n, DA_WIDTH), BF16)]
    for w in convert:
        _, rows, cols = w.shape
        n_blocks = steps if rows % (steps * 16) == 0 else steps // 2
        assert rows % (n_blocks * 16) == 0
        rb = rows // n_blocks
        blk = lambda b, h, per=steps // n_blocks: (b * DA_HEADS + h) // per
        in_specs.append(pl.BlockSpec((None, rb, cols),
                                     lambda b, h, blk=blk: (layer + 1, blk(b, h), 0)))
        out_specs.append(pl.BlockSpec((rb, cols), lambda b, h, blk=blk: (blk(b, h), 0)))
        out_shape.append(jax.ShapeDtypeStruct((rows, cols), BF16))
    outs = pl.pallas_call(
        functools.partial(_prompt_attn_kernel, lam_init=lam_init, n_convert=len(convert)),
        grid=(batch, DA_HEADS),
        in_specs=in_specs, out_specs=out_specs, out_shape=out_shape,
        scratch_shapes=[pltpu.VMEM((SEQ, MXU_DIM), BF16)],
        compiler_params=_params(2),
        name="prompt_attn",
    )(q, kb, vb, lq1, lk1, lq2, lk2, sg, *convert)
    return outs[0], outs[1:]


N_SCORE_ROWS = 2 * DA_HEADS
NEG = float(jnp.finfo(jnp.float32).min)


def _per_row_head(vec_ref):
    rid = lax.broadcasted_iota(jnp.int32, (N_SCORE_ROWS, LANES), 0)
    row_head = lax.shift_right_logical(rid, 1)
    out = jnp.zeros((N_SCORE_ROWS, LANES), F32)
    for h in range(DA_HEADS):
        hs = slice(h * LANES, (h + 1) * LANES)
        out = jnp.where(row_head == h, jnp.broadcast_to(vec_ref[:, hs], (N_SCORE_ROWS, LANES)), out)
    return out


def _decode_query(q_ref):
    rid = lax.broadcasted_iota(jnp.int32, (N_SCORE_ROWS, LANES), 0)
    lid = lax.broadcasted_iota(jnp.int32, (N_SCORE_ROWS, LANES), 1)
    own_half = lax.shift_right_logical(lid, int(math.log2(DA_QK_DIM))) == (rid & 1)
    return jnp.where(own_half, _per_row_head(q_ref), 0.0)


def _decode_query_rows(q_ref):
    rid = lax.broadcasted_iota(jnp.int32, (8, LANES), 0)
    out = jnp.zeros((8, LANES), F32)
    for h in range(DA_HEADS):
        hs = slice(h * LANES, (h + 1) * LANES)
        out = jnp.where((rid & (DA_HEADS - 1)) == h, jnp.broadcast_to(q_ref[:, hs], (8, LANES)), out)
    return out


def _decode_scores(q8, k_pages, state):
    cid = lax.broadcasted_iota(jnp.int32, (N_SCORE_ROWS, PAGE_ROWS), 1)
    rid = lax.broadcasted_iota(jnp.int32, (N_SCORE_ROWS, PAGE_ROWS), 0)
    own_col = (cid & (DA_HEADS - 1)) == lax.shift_right_logical(rid, 1)
    first_half = (rid & 1) == 0
    sub = q8.shape[0]

    def scores(kp):
        prod = (kp.reshape(PAGE_ROWS // sub, sub, LANES) * q8[None]).reshape(PAGE_ROWS, LANES)
        t = prod.T
        s1 = jnp.sum(t[0:DA_QK_DIM], axis=0, keepdims=True)
        s2 = jnp.sum(t[DA_QK_DIM:], axis=0, keepdims=True)
        return jnp.where(own_col, jnp.where(first_half, s1, s2), NEG)

    m_old, l_old, acc_old = state
    s = [scores(kp) for kp in k_pages]
    m_new = m_old
    for sp in s:
        m_new = jnp.maximum(m_new, jnp.max(sp, axis=-1, keepdims=True))
    alpha = jnp.exp2(m_old - m_new)
    l_new = alpha * l_old
    probs = []
    for sp in s:
        pr = jnp.exp2(sp - m_new)
        l_new = l_new + jnp.sum(pr, axis=-1, keepdims=True)
        probs.append(pr.astype(BF16))
    return (m_new, l_new, alpha * acc_old), probs


def _decode_values(state, probs, v_pages):
    m_new, l_new, acc = state
    for pr, vp in zip(probs, v_pages):
        acc = acc + _dot(pr, vp.astype(BF16))
    return m_new, l_new, acc


def _decode_finish(qm, state, kn_ref, vn_ref, lam, sg_ref, o_ref, lam_init):
    m_old, l_old, acc = state
    s_new = jnp.sum(qm * _per_row_head(kn_ref), axis=-1, keepdims=True)
    m_f = jnp.maximum(m_old, s_new)
    a = jnp.exp2(m_old - m_f)
    p_new = jnp.exp2(s_new - m_f)
    l_f = a * l_old + p_new
    full = (a * acc + p_new * _per_row_head(vn_ref)) / l_f
    for h in range(DA_HEADS):
        hs = slice(h * DA_V_DIM, (h + 1) * DA_V_DIM)
        o = full[2 * h:2 * h + 1, :] - lam * full[2 * h + 1:2 * h + 2, :]
        o_ref[:, hs] = (_rms(o, sg_ref[...]) * (1.0 - lam_init)).astype(o_ref.dtype)


def _finish_kernel(*refs, decode, layer, lam_init):
    if decode:
        pt_ref, refs = refs[0], refs[1:]
        (q_ref, kn_ref, vn_ref, lq1_ref, lk1_ref, lq2_ref, lk2_ref, sg_ref, ck_hbm, cv_hbm,
         o_ref, das_ref, kbuf, vbuf, sem) = refs[9:]
    else:
        (o_ref,) = refs[9:]
    x_ref, gm_ref, da_ref, wo_ref, g1_ref, g2_ref, g3_ref, wi_ref, wf_ref = refs[:9]
    vals = {}

    def project_and_norm(between_matmuls):
        mix = _dot(gm_ref[...], wo_ref[0:GM_WIDTH, :]) + _dot(da_ref[...], wo_ref[GM_WIDTH:, :])
        between_matmuls()
        vals["x1"] = x_ref[...] + _rms(mix, g1_ref[...])
        vals["xn"] = _rms(vals["x1"], g2_ref[...]).astype(BF16)

    def ffn_chunk(c0, n, last, between_matmuls):
        gate = _dot(vals["xn"], wi_ref[:, c0:c0 + n])
        up = _dot(vals["xn"], wi_ref[:, D_FF + c0:D_FF + c0 + n])
        between_matmuls()
        act = (jax.nn.silu(gate) * up).astype(BF16)
        part = _dot(act, wf_ref[c0:c0 + n, :])
        vals["f"] = part if c0 == 0 else vals["f"] + part
        if last:
            o_ref[...] = vals["x1"] + _rms(vals["f"], g3_ref[...])

    chunk_pieces = [functools.partial(ffn_chunk, c0, n, k == len(FF_CHUNKS) - 1)
                    for k, (c0, n) in enumerate(FF_CHUNKS)]

    def first_piece(between_matmuls):
        project_and_norm(between_matmuls)
        chunk_pieces[0](lambda: None)

    pieces = [first_piece] + chunk_pieces[1:]

    if not decode:
        for piece in pieces:
            piece(lambda: None)
        return

    i = pl.program_id(0)
    n_groups = len(pieces)
    n_pages = pt_ref.shape[1]
    per_group = n_pages // n_groups
    n_slots = kbuf.shape[0]
    ahead = n_slots - 1
    assert n_pages % n_groups == 0 and kbuf.shape[1] == per_group and ahead <= n_groups

    def page_copy(which, slot, p, page):
        hbm, buf = ((ck_hbm, kbuf), (cv_hbm, vbuf))[which]
        return pltpu.make_async_copy(hbm.at[layer, page], buf.at[slot, p], sem.at[which, slot])

    def start_group(row, g, slot):
        for p in range(per_group):
            page = pt_ref[row, g * per_group + p]
            page_copy(0, slot, p, page).start()
            page_copy(1, slot, p, page).start()

    def wait_group(slot):
        for p in range(per_group):
            page_copy(0, slot, p, 0).wait()
            page_copy(1, slot, p, 0).wait()

    first_slot = lax.rem(i * n_groups, n_slots)
    slot_of = lambda g: lax.rem(first_slot + g, n_slots)

    @pl.when(i == 0)
    def _():
        for g in range(ahead):
            start_group(0, g, g)

    qm = _decode_query(q_ref)
    q8 = _decode_query_rows(q_ref)
    state = (jnp.full((N_SCORE_ROWS, 1), NEG, F32), jnp.zeros((N_SCORE_ROWS, 1), F32),
             jnp.zeros((N_SCORE_ROWS, DA_V_DIM), F32))
    for g, piece in enumerate(pieces):
        slot = slot_of(g)
        if g + ahead < n_groups:
            start_group(i, g + ahead, slot_of(g + ahead))
        else:
            @pl.when(i + 1 < pl.num_programs(0))
            def _():
                start_group(i + 1, g + ahead - n_groups, slot_of(g + ahead))
        wait_group(slot)
        state, probs = _decode_scores(q8, [kbuf[slot, p] for p in range(per_group)], state)

        def add_values(state=state, probs=probs, slot=slot):
            vals["dec"] = _decode_values(state, probs, [vbuf[slot, p] for p in range(per_group)])

        piece(add_values)
        state = vals["dec"]
    lam = _lambda_vec(lq1_ref, lk1_ref, lq2_ref, lk2_ref, lam_init)
    _decode_finish(qm, state, kn_ref, vn_ref, lam, sg_ref, das_ref, lam_init)


def _finish(x, gm, da, w_o, norm_g, w_ffn_in, w_ffn_out, *, layer, tm, decode=None):
    n = x.shape[0]
    assert n % tm == 0
    steps = n // tm
    row = lambda width: pl.BlockSpec((tm, width), lambda i, *_: (i, 0))
    const = lambda shape, first: pl.BlockSpec(
        (None,) + shape, lambda i, *_: (first,) + (0,) * len(shape), pipeline_mode=pl.Buffered(1))
    gain = lambda k: const((1, D_MODEL), 4 * layer + k)
    whole = lambda shape: pl.BlockSpec(shape, lambda i, *_: (0,) * len(shape),
                                       pipeline_mode=pl.Buffered(1))
    in_specs = [row(D_MODEL), row(GM_WIDTH), row(DA_WIDTH), whole((D_MODEL, D_MODEL)),
                gain(1), gain(2), gain(3), whole((D_MODEL, 2 * D_FF)), whole((D_FF, D_MODEL))]
    args = [x, gm, da, w_o, norm_g, norm_g, norm_g, w_ffn_in, w_ffn_out]
    x_out = jax.ShapeDtypeStruct((n, D_MODEL), F32)
    if decode is None:
        return pl.pallas_call(
            functools.partial(_finish_kernel, decode=False, layer=layer, lam_init=None),
            grid=(steps,), in_specs=in_specs, out_specs=row(D_MODEL), out_shape=x_out,
            compiler_params=_params(1), name="finish_rows%d" % tm,
        )(*args)

    page_table, q, k_new, v_new, cache_k, cache_v, lams, sg, lam_init = decode
    bs, n_pages = page_table.shape
    assert bs == steps, "one sample row per grid step"
    n_groups = len(FF_CHUNKS)
    per_group = n_pages // n_groups
    rowspec = pl.BlockSpec((None, 1, DA_WIDTH), lambda i, *_: (i, 0, 0))
    in_specs += [rowspec, rowspec, rowspec] + [const((1, DA_QK_DIM), layer)] * 4 + [
        const((1, DA_V_DIM), layer), pl.BlockSpec(memory_space=pl.ANY),
        pl.BlockSpec(memory_space=pl.ANY)]
    args += [q.reshape(bs, 1, DA_WIDTH), k_new.reshape(bs, 1, DA_WIDTH),
             v_new.reshape(bs, 1, DA_WIDTH), *lams, sg, cache_k, cache_v]
    grid_spec = pltpu.PrefetchScalarGridSpec(
        num_scalar_prefetch=1, grid=(steps,), in_specs=in_specs,
        out_specs=[row(D_MODEL), rowspec],
        scratch_shapes=[pltpu.VMEM((RING_SLOTS, per_group, PAGE_ROWS, DA_V_DIM), F32),
                        pltpu.VMEM((RING_SLOTS, per_group, PAGE_ROWS, DA_V_DIM), F32),
                        pltpu.SemaphoreType.DMA((2, RING_SLOTS))],
    )
    x_new, da_s = pl.pallas_call(
        functools.partial(_finish_kernel, decode=True, layer=layer, lam_init=lam_init),
        grid_spec=grid_spec,
        out_shape=[x_out, jax.ShapeDtypeStruct((bs, 1, DA_WIDTH), BF16)],
        compiler_params=pltpu.CompilerParams(dimension_semantics=("arbitrary",),
                                             vmem_limit_bytes=RING_VMEM_LIMIT_BYTES),
        name="finish_decode",
    )(page_table, *args)
    return x_new, da_s.reshape(bs, DA_WIDTH)


def _rope_tables(pos):
    half = ROT_DIM // 2
    inv_freq = ROPE_THETA ** (-jnp.arange(0, ROT_DIM, 2, dtype=F32) / ROT_DIM)
    ang = pos.astype(F32)[:, None] * inv_freq[None, :]
    cos, sin = jnp.cos(ang), jnp.sin(ang)
    npos = pos.shape[0]
    rest = DA_QK_DIM - ROT_DIM
    ones = jnp.ones((npos, rest), F32)
    zeros = jnp.zeros((npos, rest), F32)
    zh = jnp.zeros((npos, half), F32)
    c64 = jnp.concatenate([cos, cos, ones], axis=1)
    a64 = jnp.concatenate([-sin, zh, zeros], axis=1)
    b64 = jnp.concatenate([zh, sin, zeros], axis=1)
    two = lambda t: jnp.concatenate([t, t], axis=1)
    return two(c64), two(a64), two(b64)


def kernel(x_prompt, x_sample, cache_k, cache_v, page_table, w_in, w_gmlp_s, b_gmlp_s, ln_v_g, ln_v_b,
           lambda_q1, lambda_k1, lambda_q2, lambda_k2, subln_g, w_o, norm_g, w_ffn_in, w_ffn_out):
    batch, seq, _ = x_prompt.shape
    bs = x_sample.shape[0]
    n_pool = cache_k.shape[1]
    xp = x_prompt.reshape(batch * seq, D_MODEL)
    xs = x_sample.reshape(bs, D_MODEL)
    ck = cache_k.reshape(DEPTH, n_pool, PAGE_ROWS, DA_V_DIM)
    cv = cache_v.reshape(DEPTH, n_pool, PAGE_ROWS, DA_V_DIM)
    tabs_p = _rope_tables(jnp.arange(SEQ, dtype=jnp.int32))
    tabs_s = _rope_tables(jnp.full((1,), PAST_LEN, dtype=jnp.int32))
    weights = (w_in, w_o, w_ffn_in, w_ffn_out)
    w_in_b, w_o_b, w_fi_b, w_fo_b = [w[0].astype(BF16) for w in weights]
    gains = norm_g.reshape(DEPTH * 4, 1, D_MODEL)
    lng = ln_v_g.reshape(DEPTH, 1, GM_WIDTH)
    lnb = ln_v_b.reshape(DEPTH, 1, GM_WIDTH)
    bst = jnp.swapaxes(b_gmlp_s, 1, 2)
    lams = [t.reshape(DEPTH, 1, DA_QK_DIM) for t in (lambda_q1, lambda_k1, lambda_q2, lambda_k2)]
    sg = subln_g.reshape(DEPTH, 1, DA_V_DIM)

    kv_p = None
    k_s, v_s, gv_s = [], [], []
    for l in range(DEPTH):
        lam_init = _lambda_init(l)
        gm, q, kb, vb, k_all, v_all = _mixer_in(xp, gains, w_in_b, lng, lnb, w_gmlp_s, bst, *tabs_p,
                                                layer=l, sample=False, kv_prev=kv_p)
        kv_p = (k_all, v_all)
        da, next_w = _prompt_attn(q, kb, vb, *lams, sg, layer=l, lam_init=lam_init,
                                  convert=weights if l + 1 < DEPTH else ())
        gm_s, q_s, k, v, vg = _mixer_in(xs, gains, w_in_b, lng, lnb, w_gmlp_s, bst, *tabs_s,
                                        layer=l, sample=True)
        xp, da_s = _finish(xp, gm, da, w_o_b, gains, w_fi_b, w_fo_b, layer=l, tm=ROW_TILE,
                           decode=(page_table, q_s, k, v, ck, cv, lams, sg, lam_init))
        xs = _finish(xs, gm_s, da_s, w_o_b, gains, w_fi_b, w_fo_b, layer=l, tm=bs)
        if next_w:
            w_in_b, w_o_b, w_fi_b, w_fo_b = next_w
        k_s.append(k.reshape(bs, 1, DA_HEADS, 2 * DA_QK_DIM))
        v_s.append(v.reshape(bs, 1, DA_HEADS, DA_V_DIM))
        gv_s.append(vg.reshape(bs, 1, GM_WIDTH))

    k_prompt = kv_p[0].reshape(DEPTH, batch, seq, DA_HEADS, 2 * DA_QK_DIM)
    v_prompt = kv_p[1].reshape(DEPTH, batch, seq, DA_HEADS, DA_V_DIM)
    return (xp.reshape(batch, seq, D_MODEL), xs.reshape(bs, 1, D_MODEL),
            k_prompt, v_prompt, jnp.stack(k_s), jnp.stack(v_s), jnp.stack(gv_s))
```
